```python
import jax, jax.numpy as jnp
from jax import lax
import numpy as np

D_MODEL = 2048
BATCH = 4
SEQ = 4096
DEPTH = 1

LA_QK_HEADS = 16
LA_V_HEADS = 32
LA_HEAD_DIM = 128
LA_CONV = 4
LA_CHUNK = 64
LA_QK_DIM = LA_QK_HEADS * LA_HEAD_DIM
LA_V_DIM = LA_V_HEADS * LA_HEAD_DIM
LA_CONV_DIM = 2 * LA_QK_DIM + LA_V_DIM
SWA_Q_HEADS = 32
SWA_KV_HEADS = 4
SWA_HEAD_DIM = 64
SWA_Q_DIM = SWA_Q_HEADS * SWA_HEAD_DIM
SWA_KV_DIM = SWA_KV_HEADS * SWA_HEAD_DIM
WINDOW = 128
N_EXPERTS = 32
TOP_K = 4
D_EXPERT = 2048
SWIGLU_LIMIT = 7.0
SWIGLU_ALPHA = 1.702
MOE_BLOCK = 512
NORM_EPS = 1e-5
HEAD_NORM_EPS = 1e-6

IN_SPLITS = (LA_QK_DIM, LA_QK_DIM, LA_V_DIM, LA_V_DIM, LA_V_HEADS, LA_V_HEADS,
             SWA_Q_DIM, SWA_KV_DIM, SWA_KV_DIM, D_MODEL, D_MODEL)
D_IN = sum(IN_SPLITS)
IN_OFFSETS = tuple(int(v) for v in np.cumsum(IN_SPLITS)[:-1])

kernel_name = 'hybrid_gdn_swa_moe_block'

F32 = jnp.float32


def rms_norm(x, w, eps):
    x32 = x.astype(F32)
    y = x32 * lax.rsqrt(jnp.mean(x32 * x32, axis=-1, keepdims=True) + eps)
    return (y * w.astype(F32)).astype(x.dtype)


def l2_norm(x, eps=1e-6):
    x32 = x.astype(F32)
    return x32 * lax.rsqrt(jnp.sum(x32 * x32, axis=-1, keepdims=True) + eps)


def causal_depthwise_conv(x, w):
    k_len, chans = w.shape
    return lax.conv_general_dilated(
        x, w[:, None, :].astype(x.dtype), window_strides=(1,), padding=[(k_len - 1, 0)],
        dimension_numbers=('NWC', 'WIO', 'NWC'), feature_group_count=chans)


def chunk_gated_delta_rule(q, k, v, g, beta):
    bsz, seq, heads, dk = q.shape
    dv = v.shape[-1]
    c = LA_CHUNK
    n = seq // c

    def to_chunks(t):
        t = t.astype(F32).reshape((bsz, n, c, heads) + t.shape[3:])
        return jnp.moveaxis(t, 3, 1)

    qc = to_chunks(q) * (dk ** -0.5)
    kc = to_chunks(k)
    vc = to_chunks(v)
    bc = to_chunks(beta)
    gc = jnp.cumsum(to_chunks(g), axis=-1)
    causal = jnp.tril(jnp.ones((c, c), bool))
    strict = jnp.tril(jnp.ones((c, c), bool), -1)
    decay = jnp.exp(jnp.where(causal, gc[..., :, None] - gc[..., None, :], -jnp.inf))
    k_beta = kc * bc[..., None]
    kk = jnp.einsum('bhncd,bhnsd->bhncs', k_beta, kc) * decay
    unit_lower = jnp.where(strict, kk, 0.0) + jnp.eye(c, dtype=F32)
    rhs = jnp.concatenate([vc * bc[..., None], k_beta * jnp.exp(gc)[..., None]], axis=-1)
    sol = lax.linalg.triangular_solve(unit_lower, rhs, left_side=True, lower=True, unit_diagonal=True)
    u, w = sol[..., :dv], sol[..., dv:]
    qk = jnp.einsum('bhncd,bhnsd->bhncs', qc, kc) * decay
    q_dec = qc * jnp.exp(gc)[..., None]
    k_dec = kc * jnp.exp(gc[..., -1:] - gc)[..., None]
    chunk_decay = jnp.exp(gc[..., -1])

    def step(state, xs):
        qk_i, qd_i, kd_i, u_i, w_i, cd_i = xs
        v_new = u_i - jnp.einsum('bhck,bhkv->bhcv', w_i, state)
        o_i = jnp.einsum('bhck,bhkv->bhcv', qd_i, state) + jnp.einsum('bhcs,bhsv->bhcv', qk_i, v_new)
        state = state * cd_i[..., None, None] + jnp.einsum('bhck,bhcv->bhkv', kd_i, v_new)
        return state, o_i

    xs = tuple(jnp.moveaxis(t, 2, 0) for t in (qk, q_dec, k_dec, u, w, chunk_decay))
    state0 = jnp.zeros((bsz, heads, dk, dv), F32)
    _, o = lax.scan(step, state0, xs)
    return jnp.transpose(o, (1, 0, 3, 2, 4)).reshape(bsz, seq, heads, dv)


def alibi_slopes(n_heads):
    return 2.0 ** (-8.0 * jnp.arange(1, n_heads + 1, dtype=F32) / n_heads)


def sliding_window_attention(q, k, v, sinks):
    bsz, seq, hq, dh = q.shape
    hkv = k.shape[2]
    grp = hq // hkv
    blk = WINDOW
    nb = seq // blk
    qb = q.astype(F32).reshape(bsz, nb, blk, hkv, grp, dh)
    pad = ((0, 0), (blk, 0), (0, 0), (0, 0))
    kp = jnp.pad(k.astype(F32), pad).reshape(bsz, nb + 1, blk, hkv, dh)
    vp = jnp.pad(v, pad).reshape(bsz, nb + 1, blk, hkv, dh)
    k_band = jnp.concatenate([kp[:, :-1], kp[:, 1:]], axis=2)
    v_band = jnp.concatenate([vp[:, :-1], vp[:, 1:]], axis=2)
    scores = jnp.einsum('bnqhgd,bnkhd->bhgnqk', qb, k_band) * (dh ** -0.5)
    qi = jnp.arange(blk)[:, None]
    kj = jnp.arange(2 * blk)[None, :]
    dist = qi + blk - kj
    valid = (dist >= 0) & (dist < WINDOW)
    valid = valid[None] & ((jnp.arange(nb)[:, None, None] > 0) | (kj >= blk)[None])
    slopes = alibi_slopes(hq).reshape(hkv, grp)
    scores = scores - slopes[None, :, :, None, None, None] * dist.astype(F32)
    scores = jnp.where(valid, scores, -jnp.inf)
    sink = jnp.broadcast_to(sinks.astype(F32).reshape(hkv, grp)[None, :, :, None, None, None],
                            scores.shape[:-1] + (1,))
    probs = jax.nn.softmax(jnp.concatenate([scores, sink], axis=-1), axis=-1)[..., :-1]
    out = jnp.einsum('bhgnqk,bnkhd->bnqhgd', probs.astype(v.dtype), v_band)
    return out.reshape(bsz, seq, hq * dh)


def moe_ffn(h, w_router, b_router, w1, b1, w2, b2):
    bsz, seq, d = h.shape
    n_tok = bsz * seq
    hf = h.reshape(n_tok, d)
    logits = jnp.dot(hf.astype(F32), w_router.astype(F32)) + b_router.astype(F32)
    top_vals, top_idx = lax.top_k(logits, TOP_K)
    top_w = jax.nn.softmax(top_vals, axis=-1)
    n_asg = n_tok * TOP_K
    e_flat = top_idx.reshape(n_asg)
    w_flat = top_w.reshape(n_asg)
    tok_flat = jnp.arange(n_asg, dtype=jnp.int32) // TOP_K
    order = jnp.argsort(e_flat)
    e_sorted = e_flat[order]
    counts = jnp.bincount(e_flat, length=N_EXPERTS)
    start = jnp.cumsum(counts) - counts
    padded = (counts + MOE_BLOCK - 1) // MOE_BLOCK * MOE_BLOCK
    pend = jnp.cumsum(padded)
    pstart = pend - padded
    dest = pstart[e_sorted] + jnp.arange(n_asg, dtype=jnp.int32) - start[e_sorted]
    n_blk = -(-n_asg // MOE_BLOCK) + N_EXPERTS
    n_rows = n_blk * MOE_BLOCK
    row_tok = jnp.full((n_rows,), n_tok, jnp.int32).at[dest].set(tok_flat[order])
    row_w = jnp.zeros((n_rows,), F32).at[dest].set(w_flat[order])
    blk_expert = jnp.minimum(jnp.searchsorted(pend, jnp.arange(n_blk) * MOE_BLOCK, side='right'),
                             N_EXPERTS - 1)
    x_rows = jnp.concatenate([hf, jnp.zeros((1, d), hf.dtype)], axis=0)[row_tok]
    x_rows = x_rows.reshape(n_blk, MOE_BLOCK, d)

    def expert_block(args):
        xb, e = args
        hid = xb @ w1[e] + b1[e]
        glu = jnp.minimum(hid[:, ::2], SWIGLU_LIMIT)
        lin = jnp.clip(hid[:, 1::2], -SWIGLU_LIMIT, SWIGLU_LIMIT)
        act = glu * jax.nn.sigmoid(SWIGLU_ALPHA * glu) * (lin + 1.0)
        return act @ w2[e] + b2[e]

    y_rows = lax.map(expert_block, (x_rows, blk_expert)).reshape(n_rows, d)
    y_rows = y_rows * row_w[:, None].astype(y_rows.dtype)
    out = jax.ops.segment_sum(y_rows, row_tok, num_segments=n_tok + 1)[:n_tok]
    return out.reshape(bsz, seq, d)


def setup_inputs(seed: int = 0) -> dict:
    key = jax.random.key(seed)
    ks = jax.random.split(key, 24)
    L = DEPTH

    def nrm(k, shape, fan_in):
        return jax.random.normal(k, shape, F32) * (fan_in ** -0.5)

    def gain(k, shape):
        return 1.0 + 0.02 * jax.random.normal(k, shape, F32)

    dt_min, dt_max = 0.001, 0.1
    dt = jnp.exp(jax.random.uniform(ks[6], (L, LA_V_HEADS), F32) * (np.log(dt_max) - np.log(dt_min))
                 + np.log(dt_min))
    return {
        'x': jax.random.normal(ks[0], (BATCH, SEQ, D_MODEL), F32),
        'ln_mix_w': gain(ks[1], (L, D_MODEL)),
        'w_in': nrm(ks[2], (L, D_MODEL, D_IN), D_MODEL),
        'b_gate': 0.02 * jax.random.normal(ks[3], (L, 2 * D_MODEL), F32),
        'conv_w': nrm(ks[4], (L, LA_CONV, LA_CONV_DIM), LA_CONV),
        'a_log': jnp.log(jax.random.uniform(ks[5], (L, LA_V_HEADS), F32, 1.0, 16.0)),
        'dt_bias': dt + jnp.log(-jnp.expm1(-dt)),
        'la_norm_w': gain(ks[7], (L, LA_HEAD_DIM)),
        'w_out_la': nrm(ks[8], (L, LA_V_DIM, D_MODEL), LA_V_DIM),
        'q_norm_w': gain(ks[9], (L, SWA_HEAD_DIM)),
        'k_norm_w': gain(ks[10], (L, SWA_HEAD_DIM)),
        'sinks': 0.5 * jax.random.normal(ks[11], (L, SWA_Q_HEADS), F32),
        'w_out_swa': nrm(ks[12], (L, SWA_Q_DIM, D_MODEL), SWA_Q_DIM),
        'w_o': nrm(ks[13], (L, D_MODEL, D_MODEL), D_MODEL),
        'ln_ffn_w': gain(ks[14], (L, D_MODEL)),
        'w_router': nrm(ks[15], (L, D_MODEL, N_EXPERTS), D_MODEL),
        'b_router': 0.01 * jax.random.normal(ks[16], (L, N_EXPERTS), F32),
        'w1': nrm(ks[17], (L, N_EXPERTS, D_MODEL, 2 * D_EXPERT), D_MODEL),
        'b1': 0.02 * jax.random.normal(ks[18], (L, N_EXPERTS, 2 * D_EXPERT), F32),
        'w2': nrm(ks[19], (L, N_EXPERTS, D_EXPERT, D_MODEL), D_EXPERT),
        'b2': 0.02 * jax.random.normal(ks[20], (L, N_EXPERTS, D_MODEL), F32),
    }


def reference(x, ln_mix_w, w_in, b_gate, conv_w, a_log, dt_bias, la_norm_w, w_out_la, q_norm_w,
              k_norm_w, sinks, w_out_swa, w_o, ln_ffn_w, w_router, b_router, w1, b1, w2, b2):
    bsz, seq, _ = x.shape
    for l in range(DEPTH):
        h = rms_norm(x, ln_mix_w[l], NORM_EPS)
        proj = h @ w_in[l]
        (la_q, la_k, la_v, la_z, la_b, la_a, sw_q, sw_k, sw_v,
         gate_la, gate_sw) = jnp.split(proj, IN_OFFSETS, axis=-1)

        qkv = jax.nn.silu(causal_depthwise_conv(jnp.concatenate([la_q, la_k, la_v], axis=-1), conv_w[l]))
        q_a = qkv[..., :LA_QK_DIM].reshape(bsz, seq, LA_QK_HEADS, LA_HEAD_DIM)
        k_a = qkv[..., LA_QK_DIM:2 * LA_QK_DIM].reshape(bsz, seq, LA_QK_HEADS, LA_HEAD_DIM)
        v_a = qkv[..., 2 * LA_QK_DIM:].reshape(bsz, seq, LA_V_HEADS, LA_HEAD_DIM)
        rep = LA_V_HEADS // LA_QK_HEADS
        q_a = jnp.repeat(l2_norm(q_a), rep, axis=2)
        k_a = jnp.repeat(l2_norm(k_a), rep, axis=2)
        beta = jax.nn.sigmoid(la_b.astype(F32))
        g = -jnp.exp(a_log[l].astype(F32)) * jax.nn.softplus(la_a.astype(F32) + dt_bias[l].astype(F32))
        o_a = chunk_gated_delta_rule(q_a, k_a, v_a, g, beta)
        z = la_z.astype(F32).reshape(bsz, seq, LA_V_HEADS, LA_HEAD_DIM)
        o_a = rms_norm(o_a, la_norm_w[l], HEAD_NORM_EPS) * jax.nn.silu(z)
        o_a = o_a.reshape(bsz, seq, LA_V_DIM).astype(x.dtype) @ w_out_la[l]

        q_b = rms_norm(sw_q.reshape(bsz, seq, SWA_Q_HEADS, SWA_HEAD_DIM), q_norm_w[l], HEAD_NORM_EPS)
        k_b = rms_norm(sw_k.reshape(bsz, seq, SWA_KV_HEADS, SWA_HEAD_DIM), k_norm_w[l], HEAD_NORM_EPS)
        v_b = sw_v.reshape(bsz, seq, SWA_KV_HEADS, SWA_HEAD_DIM)
        o_b = sliding_window_attention(q_b, k_b, v_b, sinks[l]) @ w_out_swa[l]

        g_la = jax.nn.sigmoid(gate_la + b_gate[l, :D_MODEL])
        g_sw = jax.nn.sigmoid(gate_sw + b_gate[l, D_MODEL:])
        x = x + (g_la * o_a + g_sw * o_b) @ w_o[l]

        h2 = rms_norm(x, ln_ffn_w[l], NORM_EPS)
        x = x + moe_ffn(h2, w_router[l], b_router[l], w1[l], b1[l], w2[l], b2[l])
    return x
```

```python
import functools

import jax
import jax.numpy as jnp
from jax import lax
from jax.experimental import pallas as pl
from jax.experimental.pallas import tpu as pltpu

F32 = jnp.float32
BF16 = jnp.bfloat16
I32 = jnp.int32

LA_QK_HEADS = 16
LA_V_HEADS = 32
LA_HEAD_DIM = 128
LA_CONV = 4
LA_CHUNK = 64
LA_QK_DIM = LA_QK_HEADS * LA_HEAD_DIM
LA_V_DIM = LA_V_HEADS * LA_HEAD_DIM
LA_CONV_DIM = 2 * LA_QK_DIM + LA_V_DIM
SWA_Q_HEADS = 32
SWA_KV_HEADS = 4
SWA_GROUP = SWA_Q_HEADS // SWA_KV_HEADS
SWA_HEAD_DIM = 64
SWA_Q_DIM = SWA_Q_HEADS * SWA_HEAD_DIM
SWA_KV_DIM = SWA_KV_HEADS * SWA_HEAD_DIM
WINDOW = 128
N_EXPERTS = 32
TOP_K = 4
SWIGLU_LIMIT = 7.0
SWIGLU_ALPHA = 1.702
NORM_EPS = 1e-5
HEAD_NORM_EPS = 1e-6
L2_EPS = 1e-6

LANES = 128
ROW_TILE = 16
MOE_ROWS = 512
VMEM_LIMIT = 56 * 1024 * 1024


def _cparams(sem, vmem=VMEM_LIMIT):
    return pltpu.CompilerParams(dimension_semantics=sem, vmem_limit_bytes=vmem)


def _rms_matmul_kernel(x_ref, g_ref, w_ref, o_ref, h_ref, *, eps, rows):
    @pl.when(pl.program_id(1) == 0)
    def _():
        for r in range(0, x_ref.shape[0], rows):
            x = x_ref[r:r + rows, :]
            var = jnp.mean(x * x, axis=-1, keepdims=True)
            h_ref[r:r + rows, :] = (x * lax.rsqrt(var + eps) * g_ref[...]).astype(BF16)

    o_ref[...] = jnp.dot(h_ref[...], w_ref[...], preferred_element_type=F32).astype(o_ref.dtype)


def rms_matmul(x, gain, w, *, out_dtype, tm, tn):
    m, k = x.shape
    n = w.shape[1]
    tm = min(tm, m)
    tn = min(tn, n)
    assert m % tm == 0 and n % tn == 0
    return pl.pallas_call(
        functools.partial(_rms_matmul_kernel, eps=NORM_EPS, rows=min(256, tm)),
        grid=(m // tm, n // tn),
        in_specs=[
            pl.BlockSpec((tm, k), lambda i, j: (i, 0)),
            pl.BlockSpec((1, k), lambda i, j: (0, 0)),
            pl.BlockSpec((k, tn), lambda i, j: (0, j)),
        ],
        out_specs=pl.BlockSpec((tm, tn), lambda i, j: (i, j)),
        out_shape=jax.ShapeDtypeStruct((m, n), out_dtype),
        scratch_shapes=[pltpu.VMEM((tm, k), BF16)],
        compiler_params=_cparams(("parallel", "arbitrary")),
        name="rms_matmul",
    )(x, gain.reshape(1, k), w)


def _conv_kernel(x_ref, halo_ref, w_ref, o_ref, buf_ref, *, tc, cw, n_q_tiles, n_qk_tiles):
    s = pl.program_id(1)
    c = pl.program_id(2)
    halo = halo_ref[0].astype(F32)
    buf_ref[0:ROW_TILE, :] = jnp.where(s > 0, halo, 0.0)
    buf_ref[ROW_TILE:ROW_TILE + tc, :] = x_ref[0].astype(F32)
    acc = jnp.zeros((tc, cw), F32)
    for j in range(LA_CONV):
        acc = acc + w_ref[j:j + 1, :] * buf_ref[pl.ds(ROW_TILE - (LA_CONV - 1) + j, tc), :]
    y = acc * jax.nn.sigmoid(acc)
    q_scale = jnp.where(c < n_q_tiles, LA_HEAD_DIM ** -0.5, 1.0)
    for hh in range(cw // LA_HEAD_DIM):
        yh = y[:, hh * LA_HEAD_DIM:(hh + 1) * LA_HEAD_DIM]
        ss = jnp.sum(yh * yh, axis=-1, keepdims=True)
        f = jnp.where(c < n_qk_tiles, lax.rsqrt(ss + L2_EPS), 1.0) * q_scale
        o_ref[0, :, hh * LA_HEAD_DIM:(hh + 1) * LA_HEAD_DIM] = (yh * f).astype(BF16)


def conv_prep(p_a, conv_w, *, tc=512, cw=512):
    b, s, _ = p_a.shape
    tc = min(tc, s)
    assert s % tc == 0 and tc % ROW_TILE == 0 and LA_QK_DIM % cw == 0
    halo_blocks = tc // ROW_TILE
    return pl.pallas_call(
        functools.partial(_conv_kernel, tc=tc, cw=cw, n_q_tiles=LA_QK_DIM // cw,
                          n_qk_tiles=2 * LA_QK_DIM // cw),
        grid=(b, s // tc, LA_CONV_DIM // cw),
        in_specs=[
            pl.BlockSpec((1, tc, cw), lambda bi, si, ci: (bi, si, ci)),
            pl.BlockSpec((1, ROW_TILE, cw),
                         lambda bi, si, ci: (bi, jnp.maximum(si * halo_blocks - 1, 0), ci)),
            pl.BlockSpec((LA_CONV, cw), lambda bi, si, ci: (0, ci)),
        ],
        out_specs=pl.BlockSpec((1, tc, cw), lambda bi, si, ci: (bi, si, ci)),
        out_shape=jax.ShapeDtypeStruct((b, s, LA_CONV_DIM), BF16),
        scratch_shapes=[pltpu.VMEM((ROW_TILE + tc, cw), F32)],
        compiler_params=_cparams(("parallel", "parallel", "parallel")),
        name="conv_prep",
    )(p_a, p_a, conv_w)


def _gates_kernel(p_ref, alog_ref, dtb_ref, beta_ref, gc_ref, *, tg):
    bproj = p_ref[:, 0:LANES]
    aproj = p_ref[:, LANES:2 * LANES]
    beta_ref[...] = jax.nn.sigmoid(bproj)
    xa = aproj + dtb_ref[...]
    softplus = jnp.maximum(xa, 0.0) + jnp.log1p(jnp.exp(-jnp.abs(xa)))
    g = -jnp.exp(alog_ref[...]) * softplus
    row = lax.broadcasted_iota(I32, (tg, tg), 0)
    col = lax.broadcasted_iota(I32, (tg, tg), 1)
    same_chunk = (row // LA_CHUNK) == (col // LA_CHUNK)
    tri = jnp.where((col <= row) & same_chunk, 1.0, 0.0).astype(F32)
    gc_ref[...] = jnp.dot(tri, g, preferred_element_type=F32, precision=lax.Precision.HIGHEST)


def gates(p_c, a_log, dt_bias, *, tg=512):
    m = p_c.shape[0]
    tg = min(tg, m)
    assert m % tg == 0 and tg % LA_CHUNK == 0
    pad = lambda v: jnp.pad(v.astype(F32), (0, LANES - v.shape[0])).reshape(1, LANES)
    return pl.pallas_call(
        functools.partial(_gates_kernel, tg=tg),
        grid=(m // tg,),
        in_specs=[
            pl.BlockSpec((tg, 2 * LANES), lambda i: (i, 0)),
            pl.BlockSpec((1, LANES), lambda i: (0, 0)),
            pl.BlockSpec((1, LANES), lambda i: (0, 0)),
        ],
        out_specs=[pl.BlockSpec((tg, LANES), lambda i: (i, 0)),
                   pl.BlockSpec((tg, LANES), lambda i: (i, 0))],
        out_shape=[jax.ShapeDtypeStruct((m, LANES), F32), jax.ShapeDtypeStruct((m, LANES), F32)],
        compiler_params=_cparams(("parallel",)),
        name="gates",
    )(p_c, pad(a_log), pad(dt_bias))


def _gdn_kernel(q_ref, k_ref, v_ref, z_ref, gcc_ref, bc_ref, gcr_ref, nw_ref, o_ref, state_ref,
                *, tb, rep):
    hq = pl.program_id(1)
    c_len = LA_CHUNK
    d = LA_HEAD_DIM

    @pl.when(pl.program_id(2) == 0)
    def _():
        state_ref[...] = jnp.zeros_like(state_ref)

    row = lax.broadcasted_iota(I32, (c_len, c_len), 0)
    col = lax.broadcasted_iota(I32, (c_len, c_len), 1)
    causal = row >= col
    strict = row > col
    eye = jnp.where(row == col, 1.0, 0.0).astype(F32)
    lane = lax.broadcasted_iota(I32, (c_len, LANES), 1)
    nw = nw_ref[...]
    hp = lax.Precision.HIGHEST
    tdot = lambda a, b_: lax.dot_general(a, b_, (((1,), (1,)), ((), ())), preferred_element_type=F32)

    def chunk(c, carry):
        r0 = pl.multiple_of(c * c_len, c_len)
        q = q_ref[0, pl.ds(r0, c_len), :]
        k = k_ref[0, pl.ds(r0, c_len), :]
        qf = q.astype(F32)
        kf = k.astype(F32)
        kk_t = tdot(k, k)
        qk_t = tdot(q, k)
        gtile = gcc_ref[0, pl.ds(r0, c_len), :]
        btile = bc_ref[0, pl.ds(r0, c_len), :]
        for j in range(rep):
            hv = hq * rep + j
            gcol = jnp.sum(jnp.where(lane == hv, gtile, 0.0), axis=1, keepdims=True)
            bcol = jnp.sum(jnp.where(lane == hv, btile, 0.0), axis=1, keepdims=True)
            grow = gcr_ref[0, j, pl.ds(c, 1), :]
            glast = gcol[c_len - 1:c_len, :]
            decay = jnp.exp(jnp.where(causal, gcol - grow, -jnp.inf))
            nmat = jnp.where(strict, -(kk_t * bcol * decay), 0.0)
            tinv = eye + nmat
            pw = nmat
            for _ in range(5):
                pw = jnp.dot(pw, pw, preferred_element_type=F32, precision=hp)
                tinv = tinv + jnp.dot(tinv, pw, preferred_element_type=F32, precision=hp)
            egc = jnp.exp(gcol)
            vf = v_ref[0, pl.ds(r0, c_len), j * d:(j + 1) * d].astype(F32)
            rhs = jnp.concatenate([vf * bcol, kf * (bcol * egc)], axis=1)
            sol = jnp.dot(tinv, rhs, preferred_element_type=F32, precision=hp)
            u = sol[:, :d]
            w = sol[:, d:]
            st = state_ref[j]
            st_b = st.astype(BF16)
            v_new = u - jnp.dot(w.astype(BF16), st_b, preferred_element_type=F32)
            q_dec = (qf * egc).astype(BF16)
            o = jnp.dot(q_dec, st_b, preferred_element_type=F32) + jnp.dot(
                (qk_t * decay).astype(BF16), v_new.astype(BF16), preferred_element_type=F32)
            k_dec = (kf * jnp.exp(glast - gcol)).astype(BF16)
            state_ref[j] = st * jnp.exp(glast) + lax.dot_general(
                k_dec, v_new.astype(BF16), (((0,), (0,)), ((), ())), preferred_element_type=F32)
            on = o * lax.rsqrt(jnp.mean(o * o, axis=-1, keepdims=True) + HEAD_NORM_EPS) * nw
            zf = z_ref[0, pl.ds(r0, c_len), j * d:(j + 1) * d].astype(F32)
            o_ref[0, pl.ds(r0, c_len), j * d:(j + 1) * d] = (on * (zf * jax.nn.sigmoid(zf))).astype(BF16)
        return carry

    lax.fori_loop(0, tb // c_len, chunk, 0)


def gdn(qkv, p_a, gc, beta, gc_rows, norm_w, *, tb=512):
    b, s, _ = qkv.shape
    tb = min(tb, s)
    assert s % tb == 0 and tb % LA_CHUNK == 0
    rep = LA_V_HEADS // LA_QK_HEADS
    d = LA_HEAD_DIM
    vw = rep * d
    return pl.pallas_call(
        functools.partial(_gdn_kernel, tb=tb, rep=rep),
        grid=(b, LA_QK_HEADS, s // tb),
        in_specs=[
            pl.BlockSpec((1, tb, d), lambda bi, h, si: (bi, si, h)),
            pl.BlockSpec((1, tb, d), lambda bi, h, si: (bi, si, LA_QK_HEADS + h)),
            pl.BlockSpec((1, tb, vw), lambda bi, h, si: (bi, si, 2 * LA_QK_DIM // vw + h)),
            pl.BlockSpec((1, tb, vw), lambda bi, h, si: (bi, si, LA_CONV_DIM // vw + h)),
            pl.BlockSpec((1, tb, LANES), lambda bi, h, si: (bi, si, 0)),
            pl.BlockSpec((1, tb, LANES), lambda bi, h, si: (bi, si, 0)),
            pl.BlockSpec((1, rep, tb // LA_CHUNK, LA_CHUNK), lambda bi, h, si: (bi, h, si, 0)),
            pl.BlockSpec((1, d), lambda bi, h, si: (0, 0)),
        ],
        out_specs=pl.BlockSpec((1, tb, vw), lambda bi, h, si: (bi, si, h)),
        out_shape=jax.ShapeDtypeStruct((b, s, LA_V_DIM), BF16),
        scratch_shapes=[pltpu.VMEM((rep, d, d), F32)],
        compiler_params=_cparams(("parallel", "parallel", "arbitrary")),
        name="gdn",
    )(qkv, qkv, qkv, p_a, gc, beta, gc_rows, norm_w.reshape(1, d).astype(F32))


def _head_rms(x, w):
    return x * lax.rsqrt(jnp.mean(x * x, axis=-1, keepdims=True) + HEAD_NORM_EPS) * w


def _swa_kernel(sink_ref, q_ref, kp_ref, kc_ref, vp_ref, vc_ref, qw_ref, kw_ref, o_ref):
    n = pl.program_id(1)
    blk = WINDOW
    dh = SWA_HEAD_DIM
    qi = lax.broadcasted_iota(I32, (blk, 2 * blk), 0)
    kj = lax.broadcasted_iota(I32, (blk, 2 * blk), 1)
    dist = qi + blk - kj
    valid = (dist >= 0) & (dist < WINDOW) & ((n > 0) | (kj >= blk))
    dist_f = dist.astype(F32)
    qw = qw_ref[...]
    kw = kw_ref[...]
    for g in range(SWA_KV_HEADS):
        sl = slice(g * dh, (g + 1) * dh)
        kb = jnp.concatenate([kp_ref[0, :, sl], kc_ref[0, :, sl]], axis=0).astype(F32)
        kb = _head_rms(kb, kw).astype(BF16)
        vb = jnp.concatenate([vp_ref[0, :, sl], vc_ref[0, :, sl]], axis=0)
        qs = []
        for j in range(SWA_GROUP):
            h = g * SWA_GROUP + j
            qh = q_ref[0, :, h * dh:(h + 1) * dh].astype(F32)
            qs.append(_head_rms(qh, qw).astype(BF16))
        qg = jnp.concatenate(qs, axis=0)
        sc = lax.dot_general(qg, kb, (((1,), (1,)), ((), ())), preferred_element_type=F32)
        ps = []
        for j in range(SWA_GROUP):
            h = g * SWA_GROUP + j
            slope = 2.0 ** (-8.0 * (h + 1) / SWA_Q_HEADS)
            sink = sink_ref[h]
            s_h = sc[j * blk:(j + 1) * blk, :] * (dh ** -0.5) - slope * dist_f
            s_h = jnp.where(valid, s_h, -jnp.inf)
            m = jnp.maximum(jnp.max(s_h, axis=-1, keepdims=True), sink)
            p = jnp.exp(s_h - m)
            denom = jnp.sum(p, axis=-1, keepdims=True) + jnp.exp(sink - m)
            ps.append((p / denom).astype(BF16))
        pg = jnp.concatenate(ps, axis=0)
        og = jnp.dot(pg, vb, preferred_element_type=F32)
        for j in range(SWA_GROUP):
            h = g * SWA_GROUP + j
            o_ref[0, :, h * dh:(h + 1) * dh] = og[j * blk:(j + 1) * blk, :].astype(BF16)


def swa(p_b, sinks, q_norm_w, k_norm_w):
    b, s, _ = p_b.shape
    blk = WINDOW
    assert s % blk == 0
    kblk = SWA_Q_DIM // SWA_KV_DIM
    prev = lambda bi, n: jnp.maximum(n - 1, 0)
    return pl.pallas_call(
        _swa_kernel,
        grid=(b, s // blk),
        in_specs=[
            pl.BlockSpec(memory_space=pltpu.SMEM),
            pl.BlockSpec((1, blk, SWA_Q_DIM), lambda bi, n: (bi, n, 0)),
            pl.BlockSpec((1, blk, SWA_KV_DIM), lambda bi, n: (bi, prev(bi, n), kblk)),
            pl.BlockSpec((1, blk, SWA_KV_DIM), lambda bi, n: (bi, n, kblk)),
            pl.BlockSpec((1, blk, SWA_KV_DIM), lambda bi, n: (bi, prev(bi, n), kblk + 1)),
            pl.BlockSpec((1, blk, SWA_KV_DIM), lambda bi, n: (bi, n, kblk + 1)),
            pl.BlockSpec((1, SWA_HEAD_DIM), lambda bi, n: (0, 0)),
            pl.BlockSpec((1, SWA_HEAD_DIM), lambda bi, n: (0, 0)),
        ],
        out_specs=pl.BlockSpec((1, blk, SWA_Q_DIM), lambda bi, n: (bi, n, 0)),
        out_shape=jax.ShapeDtypeStruct((b, s, SWA_Q_DIM), BF16),
        compiler_params=_cparams(("parallel", "parallel")),
        name="swa",
    )(sinks.astype(F32), p_b, p_b, p_b, p_b, p_b,
      q_norm_w.reshape(1, SWA_HEAD_DIM).astype(F32), k_norm_w.reshape(1, SWA_HEAD_DIM).astype(F32))


def _merge_kernel(a_ref, wla_ref, bm_ref, wsw_ref, gla_ref, gsw_ref, bla_ref, bsw_ref, o_ref):
    oa = jnp.dot(a_ref[...], wla_ref[...], preferred_element_type=F32)
    ob = jnp.dot(bm_ref[...], wsw_ref[...], preferred_element_type=F32)
    gl = jax.nn.sigmoid(gla_ref[...].astype(F32) + bla_ref[...])
    gs = jax.nn.sigmoid(gsw_ref[...].astype(F32) + bsw_ref[...])
    o_ref[...] = (gl * oa + gs * ob).astype(BF16)


def merge(o_a, w_la, o_b, w_sw, p_b, b_gate, *, tm=1024, tn=512):
    m, ka = o_a.shape
    kb = o_b.shape[1]
    dm = w_la.shape[1]
    tm = min(tm, m)
    assert m % tm == 0 and dm % tn == 0
    gate0 = (SWA_Q_DIM + 2 * SWA_KV_DIM) // tn
    assert gate0 * tn == SWA_Q_DIM + 2 * SWA_KV_DIM
    nblk = dm // tn
    bg = b_gate.astype(F32).reshape(1, 2 * dm)
    return pl.pallas_call(
        _merge_kernel,
        grid=(m // tm, nblk),
        in_specs=[
            pl.BlockSpec((tm, ka), lambda i, j: (i, 0)),
            pl.BlockSpec((ka, tn), lambda i, j: (0, j)),
            pl.BlockSpec((tm, kb), lambda i, j: (i, 0)),
            pl.BlockSpec((kb, tn), lambda i, j: (0, j)),
            pl.BlockSpec((tm, tn), lambda i, j: (i, gate0 + j)),
            pl.BlockSpec((tm, tn), lambda i, j: (i, gate0 + nblk + j)),
            pl.BlockSpec((1, tn), lambda i, j: (0, j)),
            pl.BlockSpec((1, tn), lambda i, j: (0, nblk + j)),
        ],
        out_specs=pl.BlockSpec((tm, tn), lambda i, j: (i, j)),
        out_shape=jax.ShapeDtypeStruct((m, dm), BF16),
        compiler_params=_cparams(("parallel", "arbitrary")),
        name="merge",
    )(o_a, w_la, o_b, w_sw, p_b, p_b, bg, bg)


def _oproj_router_kernel(x_ref, m_ref, wo_ref, g_ref, wr_ref, br_ref,
                         x1_ref, h2_ref, idx_ref, wgt_ref, rank_ref, cnt_ref, carry_ref, *, tm):
    @pl.when(pl.program_id(0) == 0)
    def _():
        carry_ref[...] = jnp.zeros_like(carry_ref)

    x1 = x_ref[...] + jnp.dot(m_ref[...], wo_ref[...], preferred_element_type=F32)
    x1_ref[...] = x1
    var = jnp.mean(x1 * x1, axis=-1, keepdims=True)
    h2 = x1 * lax.rsqrt(var + NORM_EPS) * g_ref[...]
    h2_ref[...] = h2.astype(BF16)
    lane = lax.broadcasted_iota(I32, (tm, LANES), 1)
    logits = jnp.dot(h2, wr_ref[...], preferred_element_type=F32,
                     precision=lax.Precision.HIGHEST) + br_ref[...]
    logits = jnp.where(lane < N_EXPERTS, logits, -jnp.inf)
    vals, idxs = [], []
    for _ in range(TOP_K):
        mx = jnp.max(logits, axis=-1, keepdims=True)
        ix = jnp.min(jnp.where(logits == mx, lane, LANES), axis=-1, keepdims=True)
        vals.append(mx)
        idxs.append(ix)
        logits = jnp.where(lane == ix, -jnp.inf, logits)
    exps = [jnp.exp(v - vals[0]) for v in vals]
    tot = exps[0]
    for e in exps[1:]:
        tot = tot + e
    multihot = jnp.zeros((tm, LANES), F32)
    for ix in idxs:
        multihot = multihot + jnp.where(lane == ix, 1.0, 0.0)
    row = lax.broadcasted_iota(I32, (tm, tm), 0)
    col = lax.broadcasted_iota(I32, (tm, tm), 1)
    lower = jnp.where(col < row, 1.0, 0.0).astype(BF16)
    before = jnp.dot(lower, multihot.astype(BF16), preferred_element_type=F32) + carry_ref[...]
    idx_o = jnp.zeros((tm, LANES), I32)
    wgt_o = jnp.zeros((tm, LANES), F32)
    rank_o = jnp.zeros((tm, LANES), I32)
    for kk in range(TOP_K):
        rk = jnp.sum(jnp.where(lane == idxs[kk], before, 0.0), axis=-1, keepdims=True)
        idx_o = jnp.where(lane == kk, idxs[kk], idx_o)
        wgt_o = jnp.where(lane == kk, exps[kk] / tot, wgt_o)
        rank_o = jnp.where(lane == kk, rk.astype(I32), rank_o)
    idx_ref[...] = idx_o
    wgt_ref[...] = wgt_o
    rank_ref[...] = rank_o
    carry = carry_ref[...] + jnp.sum(multihot, axis=0, keepdims=True)
    carry_ref[...] = carry
    cnt_ref[...] = carry.astype(I32)


def oproj_router(x, m_act, w_o, gain, w_router, b_router, *, tm=256):
    m, dm = x.shape
    tm = min(tm, m)
    assert m % tm == 0
    wr = jnp.pad(w_router.astype(F32), ((0, 0), (0, LANES - N_EXPERTS)))
    br = jnp.pad(b_router.astype(F32), (0, LANES - N_EXPERTS)).reshape(1, LANES)
    tok = lambda i: (i, 0)
    fixed = lambda i: (0, 0)
    return pl.pallas_call(
        functools.partial(_oproj_router_kernel, tm=tm),
        grid=(m // tm,),
        in_specs=[
            pl.BlockSpec((tm, dm), tok),
            pl.BlockSpec((tm, dm), tok),
            pl.BlockSpec((dm, dm), fixed),
            pl.BlockSpec((1, dm), fixed),
            pl.BlockSpec((dm, LANES), fixed),
            pl.BlockSpec((1, LANES), fixed),
        ],
        out_specs=[
            pl.BlockSpec((tm, dm), tok),
            pl.BlockSpec((tm, dm), tok),
            pl.BlockSpec((tm, LANES), tok),
            pl.BlockSpec((tm, LANES), tok),
            pl.BlockSpec((tm, LANES), tok),
            pl.BlockSpec((1, LANES), fixed),
        ],
        out_shape=[
            jax.ShapeDtypeStruct((m, dm), F32),
            jax.ShapeDtypeStruct((m, dm), BF16),
            jax.ShapeDtypeStruct((m, LANES), I32),
            jax.ShapeDtypeStruct((m, LANES), F32),
            jax.ShapeDtypeStruct((m, LANES), I32),
            jax.ShapeDtypeStruct((1, LANES), I32),
        ],
        scratch_shapes=[pltpu.VMEM((1, LANES), F32)],
        compiler_params=_cparams(("arbitrary",)),
        name="oproj_router",
    )(x, m_act, w_o, gain.reshape(1, dm).astype(F32), wr, br)


def _row_copy(src_hbm, dst_hbm, sem, src_row, dst_row):
    return pltpu.make_async_copy(src_hbm.at[src_row], dst_hbm.at[dst_row], sem)


def _dispatch_kernel(pos_ref, h_hbm, zero_hbm, xs_hbm, sem, *, td):
    del zero_hbm
    t0 = pl.program_id(0) * td

    def issue(i, carry):
        for kk in range(TOP_K):
            _row_copy(h_hbm, xs_hbm, sem, t0 + i, pos_ref[i * TOP_K + kk]).start()
        return carry

    lax.fori_loop(0, td, issue, 0)

    def drain(i, carry):
        _row_copy(h_hbm, xs_hbm, sem, 0, 0).wait()
        return carry

    lax.fori_loop(0, td * TOP_K, drain, 0)


def dispatch(h_rows, pos_flat, n_rows, *, td=256):
    m = h_rows.shape[0]
    td = min(td, m)
    assert m % td == 0
    zeros = jnp.zeros((n_rows,) + h_rows.shape[1:], h_rows.dtype)
    return pl.pallas_call(
        functools.partial(_dispatch_kernel, td=td),
        grid=(m // td,),
        in_specs=[
            pl.BlockSpec((td * TOP_K,), lambda i: (i,), memory_space=pltpu.SMEM),
            pl.BlockSpec(memory_space=pl.ANY),
            pl.BlockSpec(memory_space=pl.ANY),
        ],
        out_specs=pl.BlockSpec(memory_space=pl.ANY),
        out_shape=jax.ShapeDtypeStruct(zeros.shape, zeros.dtype),
        scratch_shapes=[pltpu.SemaphoreType.DMA(())],
        input_output_aliases={2: 0},
        compiler_params=_cparams(("arbitrary",)),
        name="dispatch",
    )(pos_flat, h_rows, zeros)


def _experts_kernel(be_ref, nv_ref, x_ref, w1g_ref, w1l_ref, b1g_ref, b1l_ref, w2_ref, b2_ref,
                    o_ref, acc_ref):
    i = pl.program_id(0)
    j = pl.program_id(1)
    nj = pl.num_programs(1)
    live = i < nv_ref[0]

    @pl.when(live)
    def _():
        x = x_ref[...]
        hg = jnp.dot(x, w1g_ref[0], preferred_element_type=F32) + b1g_ref[0]
        hl = jnp.dot(x, w1l_ref[0], preferred_element_type=F32) + b1l_ref[0]
        glu = jnp.minimum(hg, SWIGLU_LIMIT)
        lin = jnp.clip(hl, -SWIGLU_LIMIT, SWIGLU_LIMIT)
        act = glu * jax.nn.sigmoid(SWIGLU_ALPHA * glu) * (lin + 1.0)
        part = jnp.dot(act.astype(BF16), w2_ref[0], preferred_element_type=F32)

        @pl.when(j == 0)
        def _():
            acc_ref[...] = part + b2_ref[0]

        @pl.when(j > 0)
        def _():
            acc_ref[...] = acc_ref[...] + part

        @pl.when(j == nj - 1)
        def _():
            o_ref[...] = acc_ref[...].astype(o_ref.dtype)

    @pl.when(jnp.logical_not(live) & (j == nj - 1))
    def _():
        o_ref[...] = jnp.zeros_like(o_ref)


def experts(xs, blk_expert, n_live, w1g, w1l, b1g, b1l, w2, b2, *, tj=512):
    n_rows, dm = xs.shape
    n_blk = n_rows // MOE_ROWS
    de = w2.shape[1]
    assert de % tj == 0
    nj = de // tj

    def wsel(i, j, be, nv):
        return be[i], jnp.where(i < nv[0], j, nj - 1)

    grid_spec = pltpu.PrefetchScalarGridSpec(
        num_scalar_prefetch=2,
        grid=(n_blk, nj),
        in_specs=[
            pl.BlockSpec((MOE_ROWS, dm), lambda i, j, be, nv: (jnp.minimum(i, nv[0] - 1), 0)),
            pl.BlockSpec((1, dm, tj), lambda i, j, be, nv: (wsel(i, j, be, nv)[0], 0, wsel(i, j, be, nv)[1])),
            pl.BlockSpec((1, dm, tj), lambda i, j, be, nv: (wsel(i, j, be, nv)[0], 0, wsel(i, j, be, nv)[1])),
            pl.BlockSpec((1, 1, tj), lambda i, j, be, nv: (wsel(i, j, be, nv)[0], 0, wsel(i, j, be, nv)[1])),
            pl.BlockSpec((1, 1, tj), lambda i, j, be, nv: (wsel(i, j, be, nv)[0], 0, wsel(i, j, be, nv)[1])),
            pl.BlockSpec((1, tj, dm), lambda i, j, be, nv: (wsel(i, j, be, nv)[0], wsel(i, j, be, nv)[1], 0)),
            pl.BlockSpec((1, 1, dm), lambda i, j, be, nv: (be[i], 0, 0)),
        ],
        out_specs=pl.BlockSpec((MOE_ROWS, dm), lambda i, j, be, nv: (i, 0)),
        scratch_shapes=[pltpu.VMEM((MOE_ROWS, dm), F32)],
    )
    return pl.pallas_call(
        _experts_kernel,
        grid_spec=grid_spec,
        out_shape=jax.ShapeDtypeStruct((n_rows, dm), BF16),
        compiler_params=_cparams(("arbitrary", "arbitrary")),
        name="experts",
    )(blk_expert, n_live, xs, w1g, w1l, b1g, b1l, w2, b2)


def _combine_kernel(pos_ref, wgt_ref, x1_ref, y_hbm, o_ref, ybuf_ref, sem, *, tt):
    def issue(i, carry):
        _row_copy(y_hbm, ybuf_ref, sem, pos_ref[i], i).start()
        return carry

    lax.fori_loop(0, tt * TOP_K, issue, 0)

    def drain(i, carry):
        _row_copy(y_hbm, ybuf_ref, sem, 0, 0).wait()
        return carry

    lax.fori_loop(0, tt * TOP_K, drain, 0)

    def token(t, carry):
        acc = x1_ref[t]
        for kk in range(TOP_K):
            acc = acc + wgt_ref[t * TOP_K + kk] * ybuf_ref[t * TOP_K + kk].astype(F32)
        o_ref[t] = acc
        return carry

    lax.fori_loop(0, tt, token, 0)


def combine(x1_rows, y_rows, pos_flat, wgt_flat, *, tt=256):
    m = x1_rows.shape[0]
    tt = min(tt, m)
    assert m % tt == 0
    tile = x1_rows.shape[1:]
    return pl.pallas_call(
        functools.partial(_combine_kernel, tt=tt),
        grid=(m // tt,),
        in_specs=[
            pl.BlockSpec((tt * TOP_K,), lambda i: (i,), memory_space=pltpu.SMEM),
            pl.BlockSpec((tt * TOP_K,), lambda i: (i,), memory_space=pltpu.SMEM),
            pl.BlockSpec((tt,) + tile, lambda i: (i, 0, 0)),
            pl.BlockSpec(memory_space=pl.ANY),
        ],
        out_specs=pl.BlockSpec((tt,) + tile, lambda i: (i, 0, 0)),
        out_shape=jax.ShapeDtypeStruct(x1_rows.shape, F32),
        scratch_shapes=[pltpu.VMEM((tt * TOP_K,) + tile, y_rows.dtype), pltpu.SemaphoreType.DMA(())],
        compiler_params=_cparams(("arbitrary",)),
        name="combine",
    )(pos_flat, wgt_flat, x1_rows, y_rows)


def _layer(x, ln_mix_w, w_in, b_gate, conv_w, a_log, dt_bias, la_norm_w, w_out_la, q_norm_w,
           k_norm_w, sinks, w_out_swa, w_o, ln_ffn_w, w_router, b_router, w1, b1, w2, b2):
    bsz, seq, dm = x.shape
    n_tok = bsz * seq
    xf = x.reshape(n_tok, dm)

    c_z_end = LA_CONV_DIM + LA_V_DIM
    c_ba_end = c_z_end + 2 * LA_V_HEADS
    w_a = w_in[:, :c_z_end].astype(BF16)
    w_b = w_in[:, c_ba_end:].astype(BF16)
    lane_pad = ((0, 0), (0, LANES - LA_V_HEADS))
    w_c = jnp.concatenate([jnp.pad(w_in[:, c_z_end:c_z_end + LA_V_HEADS], lane_pad),
                           jnp.pad(w_in[:, c_z_end + LA_V_HEADS:c_ba_end], lane_pad)], axis=1).astype(BF16)

    p_a = rms_matmul(xf, ln_mix_w, w_a, out_dtype=BF16, tm=1024, tn=1024)
    p_b = rms_matmul(xf, ln_mix_w, w_b, out_dtype=BF16, tm=1024, tn=512)
    p_c = rms_matmul(xf, ln_mix_w, w_c, out_dtype=F32, tm=1024, tn=2 * LANES)

    p_a3 = p_a.reshape(bsz, seq, c_z_end)
    qkv = conv_prep(p_a3, conv_w)
    beta, gc = gates(p_c, a_log, dt_bias)
    gc_rows = gc[:, :LA_V_HEADS].reshape(bsz, seq // LA_CHUNK, LA_CHUNK, LA_V_HEADS).transpose(0, 3, 1, 2)
    o_a = gdn(qkv, p_a3, gc.reshape(bsz, seq, LANES), beta.reshape(bsz, seq, LANES), gc_rows, la_norm_w)

    p_b3 = p_b.reshape(bsz, seq, p_b.shape[1])
    o_b = swa(p_b3, sinks, q_norm_w, k_norm_w)

    m_act = merge(o_a.reshape(n_tok, LA_V_DIM), w_out_la.astype(BF16),
                  o_b.reshape(n_tok, SWA_Q_DIM), w_out_swa.astype(BF16), p_b, b_gate)
    x1, h2, idx, wgt, rank, counts = oproj_router(xf, m_act, w_o.astype(BF16), ln_ffn_w, w_router, b_router)

    counts = counts[0, :N_EXPERTS]
    padded = (counts + MOE_ROWS - 1) // MOE_ROWS * MOE_ROWS
    pend = jnp.cumsum(padded)
    pstart = pend - padded
    n_blk = n_tok * TOP_K // MOE_ROWS + N_EXPERTS
    n_rows = n_blk * MOE_ROWS
    blk_expert = jnp.minimum(
        jnp.searchsorted(pend, jnp.arange(n_blk, dtype=I32) * MOE_ROWS, side='right'), N_EXPERTS - 1).astype(I32)
    n_live = (pend[-1:] // MOE_ROWS).astype(I32)
    blk_expert = jnp.where(jnp.arange(n_blk) < n_live[0], blk_expert, blk_expert[n_live[0] - 1])
    idx4 = idx[:, :TOP_K]
    pos_flat = (pstart[idx4] + rank[:, :TOP_K]).reshape(n_tok * TOP_K).astype(I32)
    wgt_flat = wgt[:, :TOP_K].reshape(n_tok * TOP_K)

    tile = (dm // LANES, LANES)
    xs = dispatch(h2.reshape((n_tok,) + tile), pos_flat, n_rows)
    de = w2.shape[1]
    y = experts(xs.reshape(n_rows, dm), blk_expert, n_live,
                w1[:, :, 0::2].astype(BF16), w1[:, :, 1::2].astype(BF16),
                b1[:, 0::2].reshape(N_EXPERTS, 1, de).astype(F32), b1[:, 1::2].reshape(N_EXPERTS, 1, de).astype(F32),
                w2.astype(BF16), b2.reshape(N_EXPERTS, 1, dm).astype(F32))
    out = combine(x1.reshape((n_tok,) + tile), y.reshape((n_rows,) + tile), pos_flat, wgt_flat)
    return out.reshape(bsz, seq, dm)


def kernel(x, ln_mix_w, w_in, b_gate, conv_w, a_log, dt_bias, la_norm_w, w_out_la, q_norm_w, k_norm_w,
           sinks, w_out_swa, w_o, ln_ffn_w, w_router, b_router, w1, b1, w2, b2):
    params = (ln_mix_w, w_in, b_gate, conv_w, a_log, dt_bias, la_norm_w, w_out_la, q_norm_w, k_norm_w,
              sinks, w_out_swa, w_o, ln_ffn_w, w_router, b_router, w1, b1, w2, b2)
    for layer in range(ln_mix_w.shape[0]):
        x = _layer(x, *(p[layer] for p in params))
    return x
```

```python
import functools

import jax
import jax.numpy as jnp
from jax import lax
from jax.experimental import pallas as pl
from jax.experimental.pallas import tpu as pltpu

F32 = jnp.float32
BF16 = jnp.bfloat16
I32 = jnp.int32

LA_QK_HEADS = 16
LA_V_HEADS = 32
LA_HEAD_DIM = 128
LA_CONV = 4
LA_CHUNK = 64
LA_QK_DIM = LA_QK_HEADS * LA_HEAD_DIM
LA_V_DIM = LA_V_HEADS * LA_HEAD_DIM
LA_CONV_DIM = 2 * LA_QK_DIM + LA_V_DIM
SWA_Q_HEADS = 32
SWA_KV_HEADS = 4
SWA_GROUP = SWA_Q_HEADS // SWA_KV_HEADS
SWA_HEAD_DIM = 64
SWA_Q_DIM = SWA_Q_HEADS * SWA_HEAD_DIM
SWA_KV_DIM = SWA_KV_HEADS * SWA_HEAD_DIM
WINDOW = 128
N_EXPERTS = 32
TOP_K = 4
SWIGLU_LIMIT = 7.0
SWIGLU_ALPHA = 1.702
NORM_EPS = 1e-5
HEAD_NORM_EPS = 1e-6
L2_EPS = 1e-6

LANES = 128
ROW_TILE = 16
MOE_ROWS = 512
VMEM_LIMIT = 56 * 1024 * 1024


def _cparams(sem, vmem=VMEM_LIMIT):
    return pltpu.CompilerParams(dimension_semantics=sem, vmem_limit_bytes=vmem)


def _rms_matmul_kernel(x_ref, g_ref, w_ref, o_ref, h_ref, *, eps, rows):
    @pl.when(pl.program_id(1) == 0)
    def _():
        for r in range(0, x_ref.shape[0], rows):
            x = x_ref[r:r + rows, :]
            var = jnp.mean(x * x, axis=-1, keepdims=True)
            h_ref[r:r + rows, :] = (x * lax.rsqrt(var + eps) * g_ref[...]).astype(BF16)

    o_ref[...] = jnp.dot(h_ref[...], w_ref[...], preferred_element_type=F32).astype(o_ref.dtype)


def rms_matmul(x, gain, w, *, out_dtype, tm, tn):
    m, k = x.shape
    n = w.shape[1]
    tm = min(tm, m)
    tn = min(tn, n)
    assert m % tm == 0 and n % tn == 0
    return pl.pallas_call(
        functools.partial(_rms_matmul_kernel, eps=NORM_EPS, rows=min(256, tm)),
        grid=(m // tm, n // tn),
        in_specs=[
            pl.BlockSpec((tm, k), lambda i, j: (i, 0)),
            pl.BlockSpec((1, k), lambda i, j: (0, 0)),
            pl.BlockSpec((k, tn), lambda i, j: (0, j)),
        ],
        out_specs=pl.BlockSpec((tm, tn), lambda i, j: (i, j)),
        out_shape=jax.ShapeDtypeStruct((m, n), out_dtype),
        scratch_shapes=[pltpu.VMEM((tm, k), BF16)],
        compiler_params=_cparams(("parallel", "arbitrary")),
        name="rms_matmul",
    )(x, gain.reshape(1, k), w)


def _conv_kernel(x_ref, halo_ref, w_ref, o_ref, buf_ref, *, tc, cw, n_q_tiles, n_qk_tiles):
    s = pl.program_id(1)
    c = pl.program_id(2)
    halo = halo_ref[0].astype(F32)
    buf_ref[0:ROW_TILE, :] = jnp.where(s > 0, halo, 0.0)
    buf_ref[ROW_TILE:ROW_TILE + tc, :] = x_ref[0].astype(F32)
    acc = jnp.zeros((tc, cw), F32)
    for j in range(LA_CONV):
        acc = acc + w_ref[j:j + 1, :] * buf_ref[pl.ds(ROW_TILE - (LA_CONV - 1) + j, tc), :]
    y = acc * jax.nn.sigmoid(acc)
    q_scale = jnp.where(c < n_q_tiles, LA_HEAD_DIM ** -0.5, 1.0)
    for hh in range(cw // LA_HEAD_DIM):
        yh = y[:, hh * LA_HEAD_DIM:(hh + 1) * LA_HEAD_DIM]
        ss = jnp.sum(yh * yh, axis=-1, keepdims=True)
        f = jnp.where(c < n_qk_tiles, lax.rsqrt(ss + L2_EPS), 1.0) * q_scale
        o_ref[0, :, hh * LA_HEAD_DIM:(hh + 1) * LA_HEAD_DIM] = (yh * f).astype(BF16)


def conv_prep(p_a, conv_w, *, tc=512, cw=512):
    b, s, _ = p_a.shape
    tc = min(tc, s)
    assert s % tc == 0 and tc % ROW_TILE == 0 and LA_QK_DIM % cw == 0
    halo_blocks = tc // ROW_TILE
    return pl.pallas_call(
        functools.partial(_conv_kernel, tc=tc, cw=cw, n_q_tiles=LA_QK_DIM // cw,
                          n_qk_tiles=2 * LA_QK_DIM // cw),
        grid=(b, s // tc, LA_CONV_DIM // cw),
        in_specs=[
            pl.BlockSpec((1, tc, cw), lambda bi, si, ci: (bi, si, ci)),
            pl.BlockSpec((1, ROW_TILE, cw),
                         lambda bi, si, ci: (bi, jnp.maximum(si * halo_blocks - 1, 0), ci)),
            pl.BlockSpec((LA_CONV, cw), lambda bi, si, ci: (0, ci)),
        ],
        out_specs=pl.BlockSpec((1, tc, cw), lambda bi, si, ci: (bi, si, ci)),
        out_shape=jax.ShapeDtypeStruct((b, s, LA_CONV_DIM), BF16),
        scratch_shapes=[pltpu.VMEM((ROW_TILE + tc, cw), F32)],
        compiler_params=_cparams(("parallel", "parallel", "parallel")),
        name="conv_prep",
    )(p_a, p_a, conv_w)


def _gates_kernel(p_ref, alog_ref, dtb_ref, beta_ref, gc_ref, *, tg):
    bproj = p_ref[:, 0:LANES]
    aproj = p_ref[:, LANES:2 * LANES]
    beta_ref[...] = jax.nn.sigmoid(bproj)
    xa = aproj + dtb_ref[...]
    softplus = jnp.maximum(xa, 0.0) + jnp.log1p(jnp.exp(-jnp.abs(xa)))
    g = -jnp.exp(alog_ref[...]) * softplus
    row = lax.broadcasted_iota(I32, (tg, tg), 0)
    col = lax.broadcasted_iota(I32, (tg, tg), 1)
    same_chunk = (row // LA_CHUNK) == (col // LA_CHUNK)
    tri = jnp.where((col <= row) & same_chunk, 1.0, 0.0).astype(F32)
    gc_ref[...] = jnp.dot(tri, g, preferred_element_type=F32, precision=lax.Precision.HIGHEST)


def gates(p_c, a_log, dt_bias, *, tg=512):
    m = p_c.shape[0]
    tg = min(tg, m)
    assert m % tg == 0 and tg % LA_CHUNK == 0
    pad = lambda v: jnp.pad(v.astype(F32), (0, LANES - v.shape[0])).reshape(1, LANES)
    return pl.pallas_call(
        functools.partial(_gates_kernel, tg=tg),
        grid=(m // tg,),
        in_specs=[
            pl.BlockSpec((tg, 2 * LANES), lambda i: (i, 0)),
            pl.BlockSpec((1, LANES), lambda i: (0, 0)),
            pl.BlockSpec((1, LANES), lambda i: (0, 0)),
        ],
        out_specs=[pl.BlockSpec((tg, LANES), lambda i: (i, 0)),
                   pl.BlockSpec((tg, LANES), lambda i: (i, 0))],
        out_shape=[jax.ShapeDtypeStruct((m, LANES), F32), jax.ShapeDtypeStruct((m, LANES), F32)],
        compiler_params=_cparams(("parallel",)),
        name="gates",
    )(p_c, pad(a_log), pad(dt_bias))


def _split_bf16(a):
    hi = a.astype(BF16)
    lo = (a - hi.astype(F32)).astype(BF16)
    return hi, lo


def _dot3(a, b):
    dot = functools.partial(jnp.dot, preferred_element_type=F32)
    return dot(a[0], b[0]) + dot(a[1], b[0]) + dot(a[0], b[1])


def _gdn_kernel(q_ref, k_ref, v_ref, z_ref, gcc_ref, bc_ref, gcr_ref, nw_ref, o_ref,
                state_ref, u_ref, w_ref, aqk_ref, qd_ref, kd_ref, cd_ref, *, tb, rep, unroll):
    hq = pl.program_id(1)
    c_len = LA_CHUNK
    d = LA_HEAD_DIM
    n_chunks = tb // c_len

    @pl.when(pl.program_id(2) == 0)
    def _():
        state_ref[...] = jnp.zeros_like(state_ref)

    row = lax.broadcasted_iota(I32, (c_len, c_len), 0)
    col = lax.broadcasted_iota(I32, (c_len, c_len), 1)
    causal = row >= col
    strict = row > col
    eye = jnp.where(row == col, 1.0, 0.0).astype(F32)
    lane = lax.broadcasted_iota(I32, (c_len, LANES), 1)
    dot = functools.partial(jnp.dot, preferred_element_type=F32)
    tdot = lambda a, b_: lax.dot_general(a, b_, (((1,), (1,)), ((), ())), preferred_element_type=F32)

    def prepare_chunk(c):
        r0 = pl.multiple_of(c * c_len, c_len)
        rows = pl.ds(r0, c_len)
        q = q_ref[0, rows, :]
        k = k_ref[0, rows, :]
        qf = q.astype(F32)
        kf = k.astype(F32)
        kk_t = tdot(k, k)
        qk_t = tdot(q, k)
        gtile = gcc_ref[0, rows, :]
        btile = bc_ref[0, rows, :]
        for j in range(rep):
            hv = hq * rep + j
            gcol = jnp.sum(jnp.where(lane == hv, gtile, 0.0), axis=1, keepdims=True)
            bcol = jnp.sum(jnp.where(lane == hv, btile, 0.0), axis=1, keepdims=True)
            grow = gcr_ref[0, j, pl.ds(c, 1), :]
            glast = gcol[c_len - 1:c_len, :]
            decay = jnp.exp(jnp.where(causal, gcol - grow, -jnp.inf))
            nmat = jnp.where(strict, -(kk_t * bcol * decay), 0.0)
            tinv = eye + nmat
            pw = _split_bf16(nmat)
            for _ in range(5):
                pw = _split_bf16(_dot3(pw, pw))
                tinv = tinv + _dot3(_split_bf16(tinv), pw)
            egc = jnp.exp(gcol)
            vf = v_ref[0, rows, j * d:(j + 1) * d].astype(F32)
            rhs = jnp.concatenate([vf * bcol, kf * (bcol * egc)], axis=1).astype(BF16)
            sol = dot(tinv.astype(BF16), rhs)
            u_ref[j, rows, :] = sol[:, :d]
            w_ref[j, rows, :] = sol[:, d:].astype(BF16)
            aqk_ref[j, rows, :] = (qk_t * decay).astype(BF16)
            qd_ref[j, rows, :] = (qf * egc).astype(BF16)
            kd_ref[j, rows, :] = (kf * jnp.exp(glast - gcol)).astype(BF16)
            cd_ref[j, pl.ds(c, 1), :] = jnp.broadcast_to(jnp.exp(glast), (1, LANES))

    def prepare(g, carry):
        for i in range(unroll):
            prepare_chunk(g * unroll + i)
        return carry

    lax.fori_loop(0, n_chunks // unroll, prepare, 0)

    nw = nw_ref[...]

    def recur(c, carry):
        r0 = pl.multiple_of(c * c_len, c_len)
        rows = pl.ds(r0, c_len)
        for j in range(rep):
            st = state_ref[j]
            st_b = st.astype(BF16)
            v_new = (u_ref[j, rows, :] - dot(w_ref[j, rows, :], st_b)).astype(BF16)
            o = dot(qd_ref[j, rows, :], st_b) + dot(aqk_ref[j, rows, :], v_new)
            state_ref[j] = st * cd_ref[j, pl.ds(c, 1), :] + lax.dot_general(
                kd_ref[j, rows, :], v_new, (((0,), (0,)), ((), ())), preferred_element_type=F32)
            on = o * lax.rsqrt(jnp.mean(o * o, axis=-1, keepdims=True) + HEAD_NORM_EPS) * nw
            zf = z_ref[0, rows, j * d:(j + 1) * d].astype(F32)
            o_ref[0, rows, j * d:(j + 1) * d] = (on * (zf * jax.nn.sigmoid(zf))).astype(BF16)
        return carry

    lax.fori_loop(0, n_chunks, recur, 0)


def gdn(qkv, p_a, gc, beta, gc_rows, norm_w, *, tb=512, unroll=2):
    b, s, _ = qkv.shape
    tb = min(tb, s)
    assert s % tb == 0 and tb % (LA_CHUNK * unroll) == 0
    rep = LA_V_HEADS // LA_QK_HEADS
    d = LA_HEAD_DIM
    vw = rep * d
    n_chunks = tb // LA_CHUNK
    return pl.pallas_call(
        functools.partial(_gdn_kernel, tb=tb, rep=rep, unroll=unroll),
        grid=(b, LA_QK_HEADS, s // tb),
        in_specs=[
            pl.BlockSpec((1, tb, d), lambda bi, h, si: (bi, si, h)),
            pl.BlockSpec((1, tb, d), lambda bi, h, si: (bi, si, LA_QK_HEADS + h)),
            pl.BlockSpec((1, tb, vw), lambda bi, h, si: (bi, si, 2 * LA_QK_DIM // vw + h)),
            pl.BlockSpec((1, tb, vw), lambda bi, h, si: (bi, si, LA_CONV_DIM // vw + h)),
            pl.BlockSpec((1, tb, LANES), lambda bi, h, si: (bi, si, 0)),
            pl.BlockSpec((1, tb, LANES), lambda bi, h, si: (bi, si, 0)),
            pl.BlockSpec((1, rep, n_chunks, LA_CHUNK), lambda bi, h, si: (bi, h, si, 0)),
            pl.BlockSpec((1, d), lambda bi, h, si: (0, 0)),
        ],
        out_specs=pl.BlockSpec((1, tb, vw), lambda bi, h, si: (bi, si, h)),
        out_shape=jax.ShapeDtypeStruct((b, s, LA_V_DIM), BF16),
        scratch_shapes=[
            pltpu.VMEM((rep, d, d), F32),
            pltpu.VMEM((rep, tb, d), F32),
            pltpu.VMEM((rep, tb, d), BF16),
            pltpu.VMEM((rep, tb, LA_CHUNK), BF16),
            pltpu.VMEM((rep, tb, d), BF16),
            pltpu.VMEM((rep, tb, d), BF16),
            pltpu.VMEM((rep, n_chunks, LANES), F32),
        ],
        compiler_params=_cparams(("parallel", "parallel", "arbitrary")),
        name="gdn",
    )(qkv, qkv, qkv, p_a, gc, beta, gc_rows, norm_w.reshape(1, d).astype(F32))


def _gdn2_kernel(q_ref, k_ref, v_ref, z_ref, gcc_ref, bc_ref, gcr_ref, nw_ref, o_ref,
                 state_ref, m_ref, b_ref, qe_ref, oacc_ref, cd_ref, *, tb, hpg, rep, unroll):
    c_len = LA_CHUNK
    d = LA_HEAD_DIM
    n_chunks = tb // c_len
    hq0 = pl.program_id(1) * hpg

    @pl.when(pl.program_id(2) == 0)
    def _():
        state_ref[...] = jnp.zeros_like(state_ref)

    row = lax.broadcasted_iota(I32, (c_len, 2 * c_len), 0)
    lane = lax.broadcasted_iota(I32, (c_len, 2 * c_len), 1)
    col = jnp.where(lane < c_len, lane, lane - c_len)
    causal = row >= col
    strict = row > col
    eye_hi = jnp.where((lane >= c_len) & (row == col), 1.0, 0.0).astype(F32)
    left = lane < c_len
    dot = functools.partial(jnp.dot, preferred_element_type=F32)
    tdot = lambda a, b_: lax.dot_general(a, b_, (((1,), (1,)), ((), ())), preferred_element_type=F32)

    def prepare(g, carry):
        chains = []
        for i in range(unroll):
            c = g * unroll + i
            rows = pl.ds(pl.multiple_of(c * c_len, c_len), c_len)
            gtile = gcc_ref[0, rows, :]
            btile = bc_ref[0, rows, :]
            for a in range(hpg):
                q = q_ref[0, rows, a * d:(a + 1) * d]
                k = k_ref[0, rows, a * d:(a + 1) * d]
                kk_w = tdot(k, jnp.concatenate([k, k], axis=0))
                qk_t = tdot(q, k)
                for j in range(rep):
                    chains.append(dict(c=c, rows=rows, ch=a * rep + j, hv=(hq0 + a) * rep + j,
                                       gtile=gtile, btile=btile, q=q, k=k, kk_w=kk_w, qk_t=qk_t))
        for s in chains:
            s['gcol'] = jnp.sum(jnp.where(lane == s['hv'], s['gtile'], 0.0), axis=1, keepdims=True)
            s['bcol'] = jnp.sum(jnp.where(lane == s['hv'], s['btile'], 0.0), axis=1, keepdims=True)
            grow = gcr_ref[0, s['ch'], pl.ds(s['c'], 1), :]
            s['decay'] = jnp.exp(jnp.where(causal, s['gcol'] - grow, -jnp.inf))
            pt = jnp.where(strict & left, -(s['kk_w'] * s['bcol'] * s['decay']), 0.0)
            s['pt'] = pt + eye_hi
        for _ in range(6):
            for s in chains:
                s['ops'] = (_split_bf16(s['pt'][:, :c_len]), _split_bf16(s['pt']))
            for s in chains:
                s['x'] = _dot3(*s['ops'])
            for s in chains:
                s['pt'] = jnp.where(left, s['x'], s['pt'] + s['x'])
        for s in chains:
            tmat = pltpu.roll(s['pt'], c_len, 1)[:, :c_len].astype(BF16)
            s['egc'] = jnp.exp(s['gcol'])
            s['kf'] = s['k'].astype(F32)
            vf = v_ref[0, s['rows'], s['ch'] * d:(s['ch'] + 1) * d].astype(F32)
            rhs = jnp.concatenate([vf * s['bcol'], s['kf'] * (s['bcol'] * s['egc'])], axis=1).astype(BF16)
            s['sol'] = dot(tmat, rhs)
        for s in chains:
            sol = s['sol'].astype(BF16)
            glast = s['gcol'][c_len - 1:c_len, :]
            kd = (s['kf'] * jnp.exp(glast - s['gcol'])).astype(BF16)
            aqk = (s['qk_t'] * s['decay'][:, :c_len]).astype(BF16)
            s['big'] = lax.dot_general(kd, sol, (((0,), (0,)), ((), ())), preferred_element_type=F32)
            s['small'] = dot(aqk, sol)
            s['cd'] = jnp.exp(glast)
        for s in chains:
            ch, c, rows = s['ch'], s['c'], s['rows']
            b_ref[ch, c] = s['big'][:, :d]
            m_ref[ch, c] = s['big'][:, d:].astype(BF16)
            oacc_ref[ch, rows, :] = s['small'][:, :d]
            qe_ref[ch, rows, :] = (s['q'].astype(F32) * s['egc'] - s['small'][:, d:]).astype(BF16)
            cd_ref[ch, pl.ds(c, 1), :] = jnp.broadcast_to(s['cd'], (1, LANES))
        return carry

    lax.fori_loop(0, n_chunks // unroll, prepare, 0)

    def recur(c, carry):
        rows = pl.ds(pl.multiple_of(c * c_len, c_len), c_len)
        for ch in range(hpg * rep):
            st = state_ref[ch]
            st_b = st.astype(BF16)
            oacc_ref[ch, rows, :] = oacc_ref[ch, rows, :] + dot(qe_ref[ch, rows, :], st_b)
            state_ref[ch] = st * cd_ref[ch, pl.ds(c, 1), :] - dot(m_ref[ch, c], st_b) + b_ref[ch, c]
        return carry

    lax.fori_loop(0, n_chunks, recur, 0)

    nw = nw_ref[...]
    for ch in range(hpg * rep):
        o = oacc_ref[ch]
        on = o * lax.rsqrt(jnp.mean(o * o, axis=-1, keepdims=True) + HEAD_NORM_EPS) * nw
        zf = z_ref[0, :, ch * d:(ch + 1) * d].astype(F32)
        o_ref[0, :, ch * d:(ch + 1) * d] = (on * (zf * jax.nn.sigmoid(zf))).astype(BF16)


def gdn2(qkv, p_a, gc, beta, gc_rows, norm_w, *, tb=512, hpg=2, unroll=2):
    b, s, _ = qkv.shape
    tb = min(tb, s)
    assert s % tb == 0 and tb % (LA_CHUNK * unroll) == 0 and LA_QK_HEADS % hpg == 0
    rep = LA_V_HEADS // LA_QK_HEADS
    d = LA_HEAD_DIM
    qw = hpg * d
    vw = hpg * rep * d
    nch = hpg * rep
    n_chunks = tb // LA_CHUNK
    return pl.pallas_call(
        functools.partial(_gdn2_kernel, tb=tb, hpg=hpg, rep=rep, unroll=unroll),
        grid=(b, LA_QK_HEADS // hpg, s // tb),
        in_specs=[
            pl.BlockSpec((1, tb, qw), lambda bi, h, si: (bi, si, h)),
            pl.BlockSpec((1, tb, qw), lambda bi, h, si: (bi, si, LA_QK_DIM // qw + h)),
            pl.BlockSpec((1, tb, vw), lambda bi, h, si: (bi, si, 2 * LA_QK_DIM // vw + h)),
            pl.BlockSpec((1, tb, vw), lambda bi, h, si: (bi, si, LA_CONV_DIM // vw + h)),
            pl.BlockSpec((1, tb, LANES), lambda bi, h, si: (bi, si, 0)),
            pl.BlockSpec((1, tb, LANES), lambda bi, h, si: (bi, si, 0)),
            pl.BlockSpec((1, nch, n_chunks, 2 * LA_CHUNK), lambda bi, h, si: (bi, h, si, 0)),
            pl.BlockSpec((1, d), lambda bi, h, si: (0, 0)),
        ],
        out_specs=pl.BlockSpec((1, tb, vw), lambda bi, h, si: (bi, si, h)),
        out_shape=jax.ShapeDtypeStruct((b, s, LA_V_DIM), BF16),
        scratch_shapes=[
            pltpu.VMEM((nch, d, d), F32),
            pltpu.VMEM((nch, n_chunks, d, d), BF16),
            pltpu.VMEM((nch, n_chunks, d, d), F32),
            pltpu.VMEM((nch, tb, d), BF16),
            pltpu.VMEM((nch, tb, d), F32),
            pltpu.VMEM((nch, n_chunks, LANES), F32),
        ],
        compiler_params=_cparams(("parallel", "parallel", "arbitrary")),
        name="gdn",
    )(qkv, qkv, qkv, p_a, gc, beta, gc_rows, norm_w.reshape(1, d).astype(F32))


def _head_rms(x, w):
    return x * lax.rsqrt(jnp.mean(x * x, axis=-1, keepdims=True) + HEAD_NORM_EPS) * w


def _swa_kernel(sink_ref, q_ref, kp_ref, kc_ref, vp_ref, vc_ref, qw_ref, kw_ref, o_ref):
    n = pl.program_id(1)
    blk = WINDOW
    dh = SWA_HEAD_DIM
    qi = lax.broadcasted_iota(I32, (blk, 2 * blk), 0)
    kj = lax.broadcasted_iota(I32, (blk, 2 * blk), 1)
    dist = qi + blk - kj
    valid = (dist >= 0) & (dist < WINDOW) & ((n > 0) | (kj >= blk))
    dist_f = dist.astype(F32)
    qw = qw_ref[...]
    kw = kw_ref[...]
    for g in range(SWA_KV_HEADS):
        sl = slice(g * dh, (g + 1) * dh)
        kb = jnp.concatenate([kp_ref[0, :, sl], kc_ref[0, :, sl]], axis=0).astype(F32)
        kb = _head_rms(kb, kw).astype(BF16)
        vb = jnp.concatenate([vp_ref[0, :, sl], vc_ref[0, :, sl]], axis=0)
        qs = []
        for j in range(SWA_GROUP):
            h = g * SWA_GROUP + j
            qh = q_ref[0, :, h * dh:(h + 1) * dh].astype(F32)
            qs.append(_head_rms(qh, qw).astype(BF16))
        qg = jnp.concatenate(qs, axis=0)
        sc = lax.dot_general(qg, kb, (((1,), (1,)), ((), ())), preferred_element_type=F32)
        ps = []
        for j in range(SWA_GROUP):
            h = g * SWA_GROUP + j
            slope = 2.0 ** (-8.0 * (h + 1) / SWA_Q_HEADS)
            sink = sink_ref[h]
            s_h = sc[j * blk:(j + 1) * blk, :] * (dh ** -0.5) - slope * dist_f
            s_h = jnp.where(valid, s_h, -jnp.inf)
            m = jnp.maximum(jnp.max(s_h, axis=-1, keepdims=True), sink)
            p = jnp.exp(s_h - m)
            denom = jnp.sum(p, axis=-1, keepdims=True) + jnp.exp(sink - m)
            ps.append((p / denom).astype(BF16))
        pg = jnp.concatenate(ps, axis=0)
        og = jnp.dot(pg, vb, preferred_element_type=F32)
        for j in range(SWA_GROUP):
            h = g * SWA_GROUP + j
            o_ref[0, :, h * dh:(h + 1) * dh] = og[j * blk:(j + 1) * blk, :].astype(BF16)


def swa(p_b, sinks, q_norm_w, k_norm_w):
    b, s, _ = p_b.shape
    blk = WINDOW
    assert s % blk == 0
    kblk = SWA_Q_DIM // SWA_KV_DIM
    prev = lambda bi, n: jnp.maximum(n - 1, 0)
    return pl.pallas_call(
        _swa_kernel,
        grid=(b, s // blk),
        in_specs=[
            pl.BlockSpec(memory_space=pltpu.SMEM),
            pl.BlockSpec((1, blk, SWA_Q_DIM), lambda bi, n: (bi, n, 0)),
            pl.BlockSpec((1, blk, SWA_KV_DIM), lambda bi, n: (bi, prev(bi, n), kblk)),
            pl.BlockSpec((1, blk, SWA_KV_DIM), lambda bi, n: (bi, n, kblk)),
            pl.BlockSpec((1, blk, SWA_KV_DIM), lambda bi, n: (bi, prev(bi, n), kblk + 1)),
            pl.BlockSpec((1, blk, SWA_KV_DIM), lambda bi, n: (bi, n, kblk + 1)),
            pl.BlockSpec((1, SWA_HEAD_DIM), lambda bi, n: (0, 0)),
            pl.BlockSpec((1, SWA_HEAD_DIM), lambda bi, n: (0, 0)),
        ],
        out_specs=pl.BlockSpec((1, blk, SWA_Q_DIM), lambda bi, n: (bi, n, 0)),
        out_shape=jax.ShapeDtypeStruct((b, s, SWA_Q_DIM), BF16),
        compiler_params=_cparams(("parallel", "parallel")),
        name="swa",
    )(sinks.astype(F32), p_b, p_b, p_b, p_b, p_b,
      q_norm_w.reshape(1, SWA_HEAD_DIM).astype(F32), k_norm_w.reshape(1, SWA_HEAD_DIM).astype(F32))


def _merge_kernel(a_ref, wla_ref, bm_ref, wsw_ref, gla_ref, gsw_ref, bla_ref, bsw_ref, o_ref):
    oa = jnp.dot(a_ref[...], wla_ref[...], preferred_element_type=F32)
    ob = jnp.dot(bm_ref[...], wsw_ref[...], preferred_element_type=F32)
    gl = jax.nn.sigmoid(gla_ref[...].astype(F32) + bla_ref[...])
    gs = jax.nn.sigmoid(gsw_ref[...].astype(F32) + bsw_ref[...])
    o_ref[...] = (gl * oa + gs * ob).astype(BF16)


def merge(o_a, w_la, o_b, w_sw, p_b, b_gate, *, tm=1024, tn=512):
    m, ka = o_a.shape
    kb = o_b.shape[1]
    dm = w_la.shape[1]
    tm = min(tm, m)
    assert m % tm == 0 and dm % tn == 0
    gate0 = (SWA_Q_DIM + 2 * SWA_KV_DIM) // tn
    assert gate0 * tn == SWA_Q_DIM + 2 * SWA_KV_DIM
    nblk = dm // tn
    bg = b_gate.astype(F32).reshape(1, 2 * dm)
    return pl.pallas_call(
        _merge_kernel,
        grid=(m // tm, nblk),
        in_specs=[
            pl.BlockSpec((tm, ka), lambda i, j: (i, 0)),
            pl.BlockSpec((ka, tn), lambda i, j: (0, j)),
            pl.BlockSpec((tm, kb), lambda i, j: (i, 0)),
            pl.BlockSpec((kb, tn), lambda i, j: (0, j)),
            pl.BlockSpec((tm, tn), lambda i, j: (i, gate0 + j)),
            pl.BlockSpec((tm, tn), lambda i, j: (i, gate0 + nblk + j)),
            pl.BlockSpec((1, tn), lambda i, j: (0, j)),
            pl.BlockSpec((1, tn), lambda i, j: (0, nblk + j)),
        ],
        out_specs=pl.BlockSpec((tm, tn), lambda i, j: (i, j)),
        out_shape=jax.ShapeDtypeStruct((m, dm), BF16),
        compiler_params=_cparams(("parallel", "arbitrary")),
        name="merge",
    )(o_a, w_la, o_b, w_sw, p_b, p_b, bg, bg)


def _oproj_router_kernel(x_ref, m_ref, wo_ref, g_ref, wr_ref, br_ref,
                         x1_ref, h2_ref, idx_ref, wgt_ref, rank_ref, cnt_ref, carry_ref, *, tm):
    @pl.when(pl.program_id(0) == 0)
    def _():
        carry_ref[...] = jnp.zeros_like(carry_ref)

    x1 = x_ref[...] + jnp.dot(m_ref[...], wo_ref[...], preferred_element_type=F32)
    x1_ref[...] = x1
    var = jnp.mean(x1 * x1, axis=-1, keepdims=True)
    h2 = x1 * lax.rsqrt(var + NORM_EPS) * g_ref[...]
    h2_ref[...] = h2.astype(BF16)
    lane = lax.broadcasted_iota(I32, (tm, LANES), 1)
    logits = jnp.dot(h2, wr_ref[...], preferred_element_type=F32,
                     precision=lax.Precision.HIGHEST) + br_ref[...]
    logits = jnp.where(lane < N_EXPERTS, logits, -jnp.inf)
    vals, idxs = [], []
    for _ in range(TOP_K):
        mx = jnp.max(logits, axis=-1, keepdims=True)
        ix = jnp.min(jnp.where(logits == mx, lane, LANES), axis=-1, keepdims=True)
        vals.append(mx)
        idxs.append(ix)
        logits = jnp.where(lane == ix, -jnp.inf, logits)
    exps = [jnp.exp(v - vals[0]) for v in vals]
    tot = exps[0]
    for e in exps[1:]:
        tot = tot + e
    multihot = jnp.zeros((tm, LANES), F32)
    for ix in idxs:
        multihot = multihot + jnp.where(lane == ix, 1.0, 0.0)
    row = lax.broadcasted_iota(I32, (tm, tm), 0)
    col = lax.broadcasted_iota(I32, (tm, tm), 1)
    lower = jnp.where(col < row, 1.0, 0.0).astype(BF16)
    before = jnp.dot(lower, multihot.astype(BF16), preferred_element_type=F32) + carry_ref[...]
    idx_o = jnp.zeros((tm, LANES), I32)
    wgt_o = jnp.zeros((tm, LANES), F32)
    rank_o = jnp.zeros((tm, LANES), I32)
    for kk in range(TOP_K):
        rk = jnp.sum(jnp.where(lane == idxs[kk], before, 0.0), axis=-1, keepdims=True)
        idx_o = jnp.where(lane == kk, idxs[kk], idx_o)
        wgt_o = jnp.where(lane == kk, exps[kk] / tot, wgt_o)
        rank_o = jnp.where(lane == kk, rk.astype(I32), rank_o)
    idx_ref[...] = idx_o
    wgt_ref[...] = wgt_o
    rank_ref[...] = rank_o
    carry = carry_ref[...] + jnp.sum(multihot, axis=0, keepdims=True)
    carry_ref[...] = carry
    cnt_ref[...] = carry.astype(I32)


def oproj_router(x, m_act, w_o, gain, w_router, b_router, *, tm=256):
    m, dm = x.shape
    tm = min(tm, m)
    assert m % tm == 0
    wr = jnp.pad(w_router.astype(F32), ((0, 0), (0, LANES - N_EXPERTS)))
    br = jnp.pad(b_router.astype(F32), (0, LANES - N_EXPERTS)).reshape(1, LANES)
    tok = lambda i: (i, 0)
    fixed = lambda i: (0, 0)
    return pl.pallas_call(
        functools.partial(_oproj_router_kernel, tm=tm),
        grid=(m // tm,),
        in_specs=[
            pl.BlockSpec((tm, dm), tok),
            pl.BlockSpec((tm, dm), tok),
            pl.BlockSpec((dm, dm), fixed),
            pl.BlockSpec((1, dm), fixed),
            pl.BlockSpec((dm, LANES), fixed),
            pl.BlockSpec((1, LANES), fixed),
        ],
        out_specs=[
            pl.BlockSpec((tm, dm), tok),
            pl.BlockSpec((tm, dm), tok),
            pl.BlockSpec((tm, LANES), tok),
            pl.BlockSpec((tm, LANES), tok),
            pl.BlockSpec((tm, LANES), tok),
            pl.BlockSpec((1, LANES), fixed),
        ],
        out_shape=[
            jax.ShapeDtypeStruct((m, dm), F32),
            jax.ShapeDtypeStruct((m, dm), BF16),
            jax.ShapeDtypeStruct((m, LANES), I32),
            jax.ShapeDtypeStruct((m, LANES), F32),
            jax.ShapeDtypeStruct((m, LANES), I32),
            jax.ShapeDtypeStruct((1, LANES), I32),
        ],
        scratch_shapes=[pltpu.VMEM((1, LANES), F32)],
        compiler_params=_cparams(("arbitrary",)),
        name="oproj_router",
    )(x, m_act, w_o, gain.reshape(1, dm).astype(F32), wr, br)


def _row_copy(src_hbm, dst_hbm, sem, src_row, dst_row):
    return pltpu.make_async_copy(src_hbm.at[src_row], dst_hbm.at[dst_row], sem)


def _dispatch_kernel(pos_ref, h_ref, zero_hbm, xs_hbm, sem, *, td):
    del zero_hbm

    def issue(i, carry):
        for kk in range(TOP_K):
            _row_copy(h_ref, xs_hbm, sem, i, pos_ref[i * TOP_K + kk]).start()
        return carry

    lax.fori_loop(0, td, issue, 0)

    def drain(i, carry):
        _row_copy(h_ref, xs_hbm, sem, 0, 0).wait()
        return carry

    lax.fori_loop(0, td * TOP_K, drain, 0)


def dispatch(h_rows, pos_flat, n_rows, *, td=256):
    m = h_rows.shape[0]
    td = min(td, m)
    assert m % td == 0
    zeros = jnp.zeros((n_rows,) + h_rows.shape[1:], h_rows.dtype)
    return pl.pallas_call(
        functools.partial(_dispatch_kernel, td=td),
        grid=(m // td,),
        in_specs=[
            pl.BlockSpec((td * TOP_K,), lambda i: (i,), memory_space=pltpu.SMEM),
            pl.BlockSpec((td,) + h_rows.shape[1:], lambda i: (i, 0, 0)),
            pl.BlockSpec(memory_space=pl.ANY),
        ],
        out_specs=pl.BlockSpec(memory_space=pl.ANY),
        out_shape=jax.ShapeDtypeStruct(zeros.shape, zeros.dtype),
        scratch_shapes=[pltpu.SemaphoreType.DMA(())],
        input_output_aliases={2: 0},
        compiler_params=_cparams(("arbitrary",)),
        name="dispatch",
    )(pos_flat, h_rows, zeros)


MXU_TILE = 256


def _deinterleave_kernel(w_ref, g_ref, l_ref):
    half = MXU_TILE // 2
    r = lax.broadcasted_iota(I32, (MXU_TILE, MXU_TILE), 0)
    c = lax.broadcasted_iota(I32, (MXU_TILE, MXU_TILE), 1)
    perm = jnp.where(r == jnp.where(c < half, 2 * c, 2 * (c - half) + 1), 1.0, 0.0).astype(BF16)
    for blk in range(w_ref.shape[2] // MXU_TILE):
        piece = w_ref[0, :, blk * MXU_TILE:(blk + 1) * MXU_TILE].astype(BF16)
        out = jnp.dot(piece, perm, preferred_element_type=F32)
        g_ref[0, :, blk * half:(blk + 1) * half] = out[:, :half].astype(BF16)
        l_ref[0, :, blk * half:(blk + 1) * half] = out[:, half:].astype(BF16)


def deinterleave(w, *, tk=512, tn=1024):
    e, k, n2 = w.shape
    n = n2 // 2
    assert k % tk == 0 and n % tn == 0 and (2 * tn) % MXU_TILE == 0
    out = jax.ShapeDtypeStruct((e, k, n), BF16)
    return pl.pallas_call(
        _deinterleave_kernel,
        grid=(e, k // tk, n // tn),
        in_specs=[pl.BlockSpec((1, tk, 2 * tn), lambda ei, ki, ni: (ei, ki, ni))],
        out_specs=[pl.BlockSpec((1, tk, tn), lambda ei, ki, ni: (ei, ki, ni)),
                   pl.BlockSpec((1, tk, tn), lambda ei, ki, ni: (ei, ki, ni))],
        out_shape=[out, out],
        compiler_params=_cparams(("parallel", "parallel", "parallel")),
        name="deinterleave",
    )(w)


def _experts_kernel(be_ref, nv_ref, x_ref, w1g_ref, w1l_ref, b1g_ref, b1l_ref, w2_ref, b2_ref,
                    o_ref, acc_ref):
    i = pl.program_id(0)
    j = pl.program_id(1)
    nj = pl.num_programs(1)
    live = i < nv_ref[0]

    @pl.when(live)
    def _():
        x = x_ref[...]
        hg = jnp.dot(x, w1g_ref[0], preferred_element_type=F32) + b1g_ref[0]
        hl = jnp.dot(x, w1l_ref[0], preferred_element_type=F32) + b1l_ref[0]
        glu = jnp.minimum(hg, SWIGLU_LIMIT)
        lin = jnp.clip(hl, -SWIGLU_LIMIT, SWIGLU_LIMIT)
        act = glu * jax.nn.sigmoid(SWIGLU_ALPHA * glu) * (lin + 1.0)
        part = jnp.dot(act.astype(BF16), w2_ref[0], preferred_element_type=F32)

        @pl.when(j == 0)
        def _():
            acc_ref[...] = part + b2_ref[0]

        @pl.when(j > 0)
        def _():
            acc_ref[...] = acc_ref[...] + part

        @pl.when(j == nj - 1)
        def _():
            o_ref[...] = acc_ref[...].astype(o_ref.dtype)

    @pl.when(jnp.logical_not(live) & (j == nj - 1))
    def _():
        o_ref[...] = jnp.zeros_like(o_ref)


def experts(xs, blk_expert, n_live, w1g, w1l, b1g, b1l, w2, b2, *, tj=512):
    n_rows, dm = xs.shape
    n_blk = n_rows // MOE_ROWS
    de = w2.shape[1]
    assert de % tj == 0
    nj = de // tj

    def wsel(i, j, be, nv):
        return be[i], jnp.where(i < nv[0], j, nj - 1)

    grid_spec = pltpu.PrefetchScalarGridSpec(
        num_scalar_prefetch=2,
        grid=(n_blk, nj),
        in_specs=[
            pl.BlockSpec((MOE_ROWS, dm), lambda i, j, be, nv: (jnp.minimum(i, nv[0] - 1), 0)),
            pl.BlockSpec((1, dm, tj), lambda i, j, be, nv: (wsel(i, j, be, nv)[0], 0, wsel(i, j, be, nv)[1])),
            pl.BlockSpec((1, dm, tj), lambda i, j, be, nv: (wsel(i, j, be, nv)[0], 0, wsel(i, j, be, nv)[1])),
            pl.BlockSpec((1, 1, tj), lambda i, j, be, nv: (wsel(i, j, be, nv)[0], 0, wsel(i, j, be, nv)[1])),
            pl.BlockSpec((1, 1, tj), lambda i, j, be, nv: (wsel(i, j, be, nv)[0], 0, wsel(i, j, be, nv)[1])),
            pl.BlockSpec((1, tj, dm), lambda i, j, be, nv: (wsel(i, j, be, nv)[0], wsel(i, j, be, nv)[1], 0)),
            pl.BlockSpec((1, 1, dm), lambda i, j, be, nv: (be[i], 0, 0)),
        ],
        out_specs=pl.BlockSpec((MOE_ROWS, dm), lambda i, j, be, nv: (i, 0)),
        scratch_shapes=[pltpu.VMEM((MOE_ROWS, dm), F32)],
    )
    return pl.pallas_call(
        _experts_kernel,
        grid_spec=grid_spec,
        out_shape=jax.ShapeDtypeStruct((n_rows, dm), BF16),
        compiler_params=_cparams(("arbitrary", "arbitrary")),
        name="experts",
    )(blk_expert, n_live, xs, w1g, w1l, b1g, b1l, w2, b2)


def _combine_kernel(pos_ref, wgt_ref, x1_ref, y_hbm, o_ref, ybuf_ref, sem, *, tt):
    def issue(i, carry):
        _row_copy(y_hbm, ybuf_ref, sem, pos_ref[i], i).start()
        return carry

    lax.fori_loop(0, tt * TOP_K, issue, 0)

    def drain(i, carry):
        _row_copy(y_hbm, ybuf_ref, sem, 0, 0).wait()
        return carry

    lax.fori_loop(0, tt * TOP_K, drain, 0)

    def token(t, carry):
        acc = x1_ref[t]
        for kk in range(TOP_K):
            acc = acc + wgt_ref[t * TOP_K + kk] * ybuf_ref[t * TOP_K + kk].astype(F32)
        o_ref[t] = acc
        return carry

    lax.fori_loop(0, tt, token, 0)


def combine(x1_rows, y_rows, pos_flat, wgt_flat, *, tt=256):
    m = x1_rows.shape[0]
    tt = min(tt, m)
    assert m % tt == 0
    tile = x1_rows.shape[1:]
    return pl.pallas_call(
        functools.partial(_combine_kernel, tt=tt),
        grid=(m // tt,),
        in_specs=[
            pl.BlockSpec((tt * TOP_K,), lambda i: (i,), memory_space=pltpu.SMEM),
            pl.BlockSpec((tt * TOP_K,), lambda i: (i,), memory_space=pltpu.SMEM),
            pl.BlockSpec((tt,) + tile, lambda i: (i, 0, 0)),
            pl.BlockSpec(memory_space=pl.ANY),
        ],
        out_specs=pl.BlockSpec((tt,) + tile, lambda i: (i, 0, 0)),
        out_shape=jax.ShapeDtypeStruct(x1_rows.shape, F32),
        scratch_shapes=[pltpu.VMEM((tt * TOP_K,) + tile, y_rows.dtype), pltpu.SemaphoreType.DMA(())],
        compiler_params=_cparams(("arbitrary",)),
        name="combine",
    )(pos_flat, wgt_flat, x1_rows, y_rows)


def _layer(x, ln_mix_w, w_in, b_gate, conv_w, a_log, dt_bias, la_norm_w, w_out_la, q_norm_w,
           k_norm_w, sinks, w_out_swa, w_o, ln_ffn_w, w_router, b_router, w1, b1, w2, b2):
    bsz, seq, dm = x.shape
    n_tok = bsz * seq
    xf = x.reshape(n_tok, dm)

    c_z_end = LA_CONV_DIM + LA_V_DIM
    c_ba_end = c_z_end + 2 * LA_V_HEADS
    w_a = w_in[:, :c_z_end].astype(BF16)
    w_b = w_in[:, c_ba_end:].astype(BF16)
    lane_pad = ((0, 0), (0, LANES - LA_V_HEADS))
    w_c = jnp.concatenate([jnp.pad(w_in[:, c_z_end:c_z_end + LA_V_HEADS], lane_pad),
                           jnp.pad(w_in[:, c_z_end + LA_V_HEADS:c_ba_end], lane_pad)], axis=1).astype(BF16)

    p_a = rms_matmul(xf, ln_mix_w, w_a, out_dtype=BF16, tm=1024, tn=1024)
    p_b = rms_matmul(xf, ln_mix_w, w_b, out_dtype=BF16, tm=1024, tn=512)
    p_c = rms_matmul(xf, ln_mix_w, w_c, out_dtype=F32, tm=1024, tn=2 * LANES)

    p_a3 = p_a.reshape(bsz, seq, c_z_end)
    qkv = conv_prep(p_a3, conv_w)
    beta, gc = gates(p_c, a_log, dt_bias)
    gc_rows = gc[:, :LA_V_HEADS].reshape(bsz, seq // LA_CHUNK, LA_CHUNK, LA_V_HEADS).transpose(0, 3, 1, 2)
    gc_rows = jnp.concatenate([gc_rows, gc_rows], axis=-1)
    o_a = gdn2(qkv, p_a3, gc.reshape(bsz, seq, LANES), beta.reshape(bsz, seq, LANES), gc_rows, la_norm_w)

    p_b3 = p_b.reshape(bsz, seq, p_b.shape[1])
    o_b = swa(p_b3, sinks, q_norm_w, k_norm_w)

    m_act = merge(o_a.reshape(n_tok, LA_V_DIM), w_out_la.astype(BF16),
                  o_b.reshape(n_tok, SWA_Q_DIM), w_out_swa.astype(BF16), p_b, b_gate)
    x1, h2, idx, wgt, rank, counts = oproj_router(xf, m_act, w_o.astype(BF16), ln_ffn_w, w_router, b_router)

    counts = counts[0, :N_EXPERTS]
    padded = (counts + MOE_ROWS - 1) // MOE_ROWS * MOE_ROWS
    pend = jnp.cumsum(padded)
    pstart = pend - padded
    n_blk = n_tok * TOP_K // MOE_ROWS + N_EXPERTS
    n_rows = n_blk * MOE_ROWS
    blk_expert = jnp.minimum(
        jnp.searchsorted(pend, jnp.arange(n_blk, dtype=I32) * MOE_ROWS, side='right'), N_EXPERTS - 1).astype(I32)
    n_live = (pend[-1:] // MOE_ROWS).astype(I32)
    blk_expert = jnp.where(jnp.arange(n_blk) < n_live[0], blk_expert, blk_expert[n_live[0] - 1])
    idx4 = idx[:, :TOP_K]
    pos_flat = (pstart[idx4] + rank[:, :TOP_K]).reshape(n_tok * TOP_K).astype(I32)
    wgt_flat = wgt[:, :TOP_K].reshape(n_tok * TOP_K)

    tile = (dm // LANES, LANES)
    xs = dispatch(h2.reshape((n_tok,) + tile), pos_flat, n_rows)
    de = w2.shape[1]
    w1g, w1l = deinterleave(w1)
    y = experts(xs.reshape(n_rows, dm), blk_expert, n_live, w1g, w1l,
                b1[:, 0::2].reshape(N_EXPERTS, 1, de).astype(F32), b1[:, 1::2].reshape(N_EXPERTS, 1, de).astype(F32),
                w2.astype(BF16), b2.reshape(N_EXPERTS, 1, dm).astype(F32))
    out = combine(x1.reshape((n_tok,) + tile), y.reshape((n_rows,) + tile), pos_flat, wgt_flat)
    return out.reshape(bsz, seq, dm)


def kernel(x, ln_mix_w, w_in, b_gate, conv_w, a_log, dt_bias, la_norm_w, w_out_la, q_norm_w, k_norm_w,
           sinks, w_out_swa, w_o, ln_ffn_w, w_router, b_router, w1, b1, w2, b2):
    params = (ln_mix_w, w_in, b_gate, conv_w, a_log, dt_bias, la_norm_w, w_out_la, q_norm_w, k_norm_w,
              sinks, w_out_swa, w_o, ln_ffn_w, w_router, b_router, w1, b1, w2, b2)
    for layer in range(ln_mix_w.shape[0]):
        x = _layer(x, *(p[layer] for p in params))
    return x
```

```python
import functools

import jax
import jax.numpy as jnp
from jax import lax
from jax.experimental import pallas as pl
from jax.experimental.pallas import tpu as pltpu

F32 = jnp.float32
BF16 = jnp.bfloat16
I32 = jnp.int32

LA_QK_HEADS = 16
LA_V_HEADS = 32
LA_HEAD_DIM = 128
LA_CONV = 4
LA_CHUNK = 64
LA_QK_DIM = LA_QK_HEADS * LA_HEAD_DIM
LA_V_DIM = LA_V_HEADS * LA_HEAD_DIM
LA_CONV_DIM = 2 * LA_QK_DIM + LA_V_DIM
SWA_Q_HEADS = 32
SWA_KV_HEADS = 4
SWA_GROUP = SWA_Q_HEADS // SWA_KV_HEADS
SWA_HEAD_DIM = 64
SWA_Q_DIM = SWA_Q_HEADS * SWA_HEAD_DIM
SWA_KV_DIM = SWA_KV_HEADS * SWA_HEAD_DIM
WINDOW = 128
N_EXPERTS = 32
TOP_K = 4
SWIGLU_LIMIT = 7.0
SWIGLU_ALPHA = 1.702
NORM_EPS = 1e-5
HEAD_NORM_EPS = 1e-6
L2_EPS = 1e-6

LANES = 128
ROW_TILE = 16
MOE_ROWS = 512
VMEM_LIMIT = 56 * 1024 * 1024


def _cparams(sem, vmem=VMEM_LIMIT):
    return pltpu.CompilerParams(dimension_semantics=sem, vmem_limit_bytes=vmem)


def _rms_matmul_kernel(x_ref, g_ref, w_ref, o_ref, h_ref, *, eps, rows):
    @pl.when(pl.program_id(1) == 0)
    def _():
        for r in range(0, x_ref.shape[0], rows):
            x = x_ref[r:r + rows, :]
            var = jnp.mean(x * x, axis=-1, keepdims=True)
            h_ref[r:r + rows, :] = (x * lax.rsqrt(var + eps) * g_ref[...]).astype(BF16)

    o_ref[...] = jnp.dot(h_ref[...], w_ref[...], preferred_element_type=F32).astype(o_ref.dtype)


def rms_matmul(x, gain, w, *, out_dtype, tm, tn):
    m, k = x.shape
    n = w.shape[1]
    tm = min(tm, m)
    tn = min(tn, n)
    assert m % tm == 0 and n % tn == 0
    return pl.pallas_call(
        functools.partial(_rms_matmul_kernel, eps=NORM_EPS, rows=min(256, tm)),
        grid=(m // tm, n // tn),
        in_specs=[
            pl.BlockSpec((tm, k), lambda i, j: (i, 0)),
            pl.BlockSpec((1, k), lambda i, j: (0, 0)),
            pl.BlockSpec((k, tn), lambda i, j: (0, j)),
        ],
        out_specs=pl.BlockSpec((tm, tn), lambda i, j: (i, j)),
        out_shape=jax.ShapeDtypeStruct((m, n), out_dtype),
        scratch_shapes=[pltpu.VMEM((tm, k), BF16)],
        compiler_params=_cparams(("parallel", "arbitrary")),
        name="rms_matmul",
    )(x, gain.reshape(1, k), w)


def _conv_kernel(x_ref, halo_ref, w_ref, o_ref, buf_ref, *, tc, cw, n_q_tiles, n_qk_tiles):
    s = pl.program_id(1)
    c = pl.program_id(2)
    halo = halo_ref[0].astype(F32)
    buf_ref[0:ROW_TILE, :] = jnp.where(s > 0, halo, 0.0)
    buf_ref[ROW_TILE:ROW_TILE + tc, :] = x_ref[0].astype(F32)
    acc = jnp.zeros((tc, cw), F32)
    for j in range(LA_CONV):
        acc = acc + w_ref[j:j + 1, :] * buf_ref[pl.ds(ROW_TILE - (LA_CONV - 1) + j, tc), :]
    y = acc * jax.nn.sigmoid(acc)
    q_scale = jnp.where(c < n_q_tiles, LA_HEAD_DIM ** -0.5, 1.0)
    for hh in range(cw // LA_HEAD_DIM):
        yh = y[:, hh * LA_HEAD_DIM:(hh + 1) * LA_HEAD_DIM]
        ss = jnp.sum(yh * yh, axis=-1, keepdims=True)
        f = jnp.where(c < n_qk_tiles, lax.rsqrt(ss + L2_EPS), 1.0) * q_scale
        o_ref[0, :, hh * LA_HEAD_DIM:(hh + 1) * LA_HEAD_DIM] = (yh * f).astype(BF16)


def conv_prep(p_a, conv_w, *, tc=512, cw=512):
    b, s, _ = p_a.shape
    tc = min(tc, s)
    assert s % tc == 0 and tc % ROW_TILE == 0 and LA_QK_DIM % cw == 0
    halo_blocks = tc // ROW_TILE
    return pl.pallas_call(
        functools.partial(_conv_kernel, tc=tc, cw=cw, n_q_tiles=LA_QK_DIM // cw,
                          n_qk_tiles=2 * LA_QK_DIM // cw),
        grid=(b, s // tc, LA_CONV_DIM // cw),
        in_specs=[
            pl.BlockSpec((1, tc, cw), lambda bi, si, ci: (bi, si, ci)),
            pl.BlockSpec((1, ROW_TILE, cw),
                         lambda bi, si, ci: (bi, jnp.maximum(si * halo_blocks - 1, 0), ci)),
            pl.BlockSpec((LA_CONV, cw), lambda bi, si, ci: (0, ci)),
        ],
        out_specs=pl.BlockSpec((1, tc, cw), lambda bi, si, ci: (bi, si, ci)),
        out_shape=jax.ShapeDtypeStruct((b, s, LA_CONV_DIM), BF16),
        scratch_shapes=[pltpu.VMEM((ROW_TILE + tc, cw), F32)],
        compiler_params=_cparams(("parallel", "parallel", "parallel")),
        name="conv_prep",
    )(p_a, p_a, conv_w)


def _gates_kernel(p_ref, alog_ref, dtb_ref, beta_ref, gc_ref, *, tg):
    bproj = p_ref[:, 0:LANES]
    aproj = p_ref[:, LANES:2 * LANES]
    beta_ref[...] = jax.nn.sigmoid(bproj)
    xa = aproj + dtb_ref[...]
    softplus = jnp.maximum(xa, 0.0) + jnp.log1p(jnp.exp(-jnp.abs(xa)))
    g = -jnp.exp(alog_ref[...]) * softplus
    row = lax.broadcasted_iota(I32, (tg, tg), 0)
    col = lax.broadcasted_iota(I32, (tg, tg), 1)
    same_chunk = (row // LA_CHUNK) == (col // LA_CHUNK)
    tri = jnp.where((col <= row) & same_chunk, 1.0, 0.0).astype(F32)
    gc_ref[...] = jnp.dot(tri, g, preferred_element_type=F32, precision=lax.Precision.HIGHEST)


def gates(p_c, a_log, dt_bias, *, tg=512):
    m = p_c.shape[0]
    tg = min(tg, m)
    assert m % tg == 0 and tg % LA_CHUNK == 0
    pad = lambda v: jnp.pad(v.astype(F32), (0, LANES - v.shape[0])).reshape(1, LANES)
    return pl.pallas_call(
        functools.partial(_gates_kernel, tg=tg),
        grid=(m // tg,),
        in_specs=[
            pl.BlockSpec((tg, 2 * LANES), lambda i: (i, 0)),
            pl.BlockSpec((1, LANES), lambda i: (0, 0)),
            pl.BlockSpec((1, LANES), lambda i: (0, 0)),
        ],
        out_specs=[pl.BlockSpec((tg, LANES), lambda i: (i, 0)),
                   pl.BlockSpec((tg, LANES), lambda i: (i, 0))],
        out_shape=[jax.ShapeDtypeStruct((m, LANES), F32), jax.ShapeDtypeStruct((m, LANES), F32)],
        compiler_params=_cparams(("parallel",)),
        name="gates",
    )(p_c, pad(a_log), pad(dt_bias))


def _split_bf16(a):
    hi = a.astype(BF16)
    lo = (a - hi.astype(F32)).astype(BF16)
    return hi, lo


def _dot3(a, b):
    dot = functools.partial(jnp.dot, preferred_element_type=F32)
    return dot(a[0], b[0]) + dot(a[1], b[0]) + dot(a[0], b[1])


def _gdn_kernel(q_ref, k_ref, v_ref, z_ref, gcc_ref, bc_ref, gcr_ref, nw_ref, o_ref,
                state_ref, u_ref, w_ref, aqk_ref, qd_ref, kd_ref, cd_ref, *, tb, rep, unroll):
    hq = pl.program_id(1)
    c_len = LA_CHUNK
    d = LA_HEAD_DIM
    n_chunks = tb // c_len

    @pl.when(pl.program_id(2) == 0)
    def _():
        state_ref[...] = jnp.zeros_like(state_ref)

    row = lax.broadcasted_iota(I32, (c_len, c_len), 0)
    col = lax.broadcasted_iota(I32, (c_len, c_len), 1)
    causal = row >= col
    strict = row > col
    eye = jnp.where(row == col, 1.0, 0.0).astype(F32)
    lane = lax.broadcasted_iota(I32, (c_len, LANES), 1)
    dot = functools.partial(jnp.dot, preferred_element_type=F32)
    tdot = lambda a, b_: lax.dot_general(a, b_, (((1,), (1,)), ((), ())), preferred_element_type=F32)

    def prepare_chunk(c):
        r0 = pl.multiple_of(c * c_len, c_len)
        rows = pl.ds(r0, c_len)
        q = q_ref[0, rows, :]
        k = k_ref[0, rows, :]
        qf = q.astype(F32)
        kf = k.astype(F32)
        kk_t = tdot(k, k)
        qk_t = tdot(q, k)
        gtile = gcc_ref[0, rows, :]
        btile = bc_ref[0, rows, :]
        for j in range(rep):
            hv = hq * rep + j
            gcol = jnp.sum(jnp.where(lane == hv, gtile, 0.0), axis=1, keepdims=True)
            bcol = jnp.sum(jnp.where(lane == hv, btile, 0.0), axis=1, keepdims=True)
            grow = gcr_ref[0, j, pl.ds(c, 1), :]
            glast = gcol[c_len - 1:c_len, :]
            decay = jnp.exp(jnp.where(causal, gcol - grow, -jnp.inf))
            nmat = jnp.where(strict, -(kk_t * bcol * decay), 0.0)
            tinv = eye + nmat
            pw = _split_bf16(nmat)
            for _ in range(5):
                pw = _split_bf16(_dot3(pw, pw))
                tinv = tinv + _dot3(_split_bf16(tinv), pw)
            egc = jnp.exp(gcol)
            vf = v_ref[0, rows, j * d:(j + 1) * d].astype(F32)
            rhs = jnp.concatenate([vf * bcol, kf * (bcol * egc)], axis=1).astype(BF16)
            sol = dot(tinv.astype(BF16), rhs)
            u_ref[j, rows, :] = sol[:, :d]
            w_ref[j, rows, :] = sol[:, d:].astype(BF16)
            aqk_ref[j, rows, :] = (qk_t * decay).astype(BF16)
            qd_ref[j, rows, :] = (qf * egc).astype(BF16)
            kd_ref[j, rows, :] = (kf * jnp.exp(glast - gcol)).astype(BF16)
            cd_ref[j, pl.ds(c, 1), :] = jnp.broadcast_to(jnp.exp(glast), (1, LANES))

    def prepare(g, carry):
        for i in range(unroll):
            prepare_chunk(g * unroll + i)
        return carry

    lax.fori_loop(0, n_chunks // unroll, prepare, 0)

    nw = nw_ref[...]

    def recur(c, carry):
        r0 = pl.multiple_of(c * c_len, c_len)
        rows = pl.ds(r0, c_len)
        for j in range(rep):
            st = state_ref[j]
            st_b = st.astype(BF16)
            v_new = (u_ref[j, rows, :] - dot(w_ref[j, rows, :], st_b)).astype(BF16)
            o = dot(qd_ref[j, rows, :], st_b) + dot(aqk_ref[j, rows, :], v_new)
            state_ref[j] = st * cd_ref[j, pl.ds(c, 1), :] + lax.dot_general(
                kd_ref[j, rows, :], v_new, (((0,), (0,)), ((), ())), preferred_element_type=F32)
            on = o * lax.rsqrt(jnp.mean(o * o, axis=-1, keepdims=True) + HEAD_NORM_EPS) * nw
            zf = z_ref[0, rows, j * d:(j + 1) * d].astype(F32)
            o_ref[0, rows, j * d:(j + 1) * d] = (on * (zf * jax.nn.sigmoid(zf))).astype(BF16)
        return carry

    lax.fori_loop(0, n_chunks, recur, 0)


def gdn(qkv, p_a, gc, beta, gc_rows, norm_w, *, tb=512, unroll=2):
    b, s, _ = qkv.shape
    tb = min(tb, s)
    assert s % tb == 0 and tb % (LA_CHUNK * unroll) == 0
    rep = LA_V_HEADS // LA_QK_HEADS
    d = LA_HEAD_DIM
    vw = rep * d
    n_chunks = tb // LA_CHUNK
    return pl.pallas_call(
        functools.partial(_gdn_kernel, tb=tb, rep=rep, unroll=unroll),
        grid=(b, LA_QK_HEADS, s // tb),
        in_specs=[
            pl.BlockSpec((1, tb, d), lambda bi, h, si: (bi, si, h)),
            pl.BlockSpec((1, tb, d), lambda bi, h, si: (bi, si, LA_QK_HEADS + h)),
            pl.BlockSpec((1, tb, vw), lambda bi, h, si: (bi, si, 2 * LA_QK_DIM // vw + h)),
            pl.BlockSpec((1, tb, vw), lambda bi, h, si: (bi, si, LA_CONV_DIM // vw + h)),
            pl.BlockSpec((1, tb, LANES), lambda bi, h, si: (bi, si, 0)),
            pl.BlockSpec((1, tb, LANES), lambda bi, h, si: (bi, si, 0)),
            pl.BlockSpec((1, rep, n_chunks, LA_CHUNK), lambda bi, h, si: (bi, h, si, 0)),
            pl.BlockSpec((1, d), lambda bi, h, si: (0, 0)),
        ],
        out_specs=pl.BlockSpec((1, tb, vw), lambda bi, h, si: (bi, si, h)),
        out_shape=jax.ShapeDtypeStruct((b, s, LA_V_DIM), BF16),
        scratch_shapes=[
            pltpu.VMEM((rep, d, d), F32),
            pltpu.VMEM((rep, tb, d), F32),
            pltpu.VMEM((rep, tb, d), BF16),
            pltpu.VMEM((rep, tb, LA_CHUNK), BF16),
            pltpu.VMEM((rep, tb, d), BF16),
            pltpu.VMEM((rep, tb, d), BF16),
            pltpu.VMEM((rep, n_chunks, LANES), F32),
        ],
        compiler_params=_cparams(("parallel", "parallel", "arbitrary")),
        name="gdn",
    )(qkv, qkv, qkv, p_a, gc, beta, gc_rows, norm_w.reshape(1, d).astype(F32))


def _gdn2_kernel(q_ref, k_ref, v_ref, z_ref, gcc_ref, bc_ref, gcr_ref, nw_ref, o_ref,
                 state_ref, m_ref, b_ref, qe_ref, oacc_ref, cd_ref, *, tb, hpg, rep, unroll):
    c_len = LA_CHUNK
    d = LA_HEAD_DIM
    n_chunks = tb // c_len
    hq0 = pl.program_id(1) * hpg

    @pl.when(pl.program_id(2) == 0)
    def _():
        state_ref[...] = jnp.zeros_like(state_ref)

    row = lax.broadcasted_iota(I32, (c_len, 2 * c_len), 0)
    lane = lax.broadcasted_iota(I32, (c_len, 2 * c_len), 1)
    col = jnp.where(lane < c_len, lane, lane - c_len)
    causal = row >= col
    strict = row > col
    eye_hi = jnp.where((lane >= c_len) & (row == col), 1.0, 0.0).astype(F32)
    left = lane < c_len
    dot = functools.partial(jnp.dot, preferred_element_type=F32)
    tdot = lambda a, b_: lax.dot_general(a, b_, (((1,), (1,)), ((), ())), preferred_element_type=F32)

    def prepare(g, carry):
        chains = []
        for i in range(unroll):
            c = g * unroll + i
            rows = pl.ds(pl.multiple_of(c * c_len, c_len), c_len)
            gtile = gcc_ref[0, rows, :]
            btile = bc_ref[0, rows, :]
            for a in range(hpg):
                q = q_ref[0, rows, a * d:(a + 1) * d]
                k = k_ref[0, rows, a * d:(a + 1) * d]
                kk_w = tdot(k, jnp.concatenate([k, k], axis=0))
                qk_t = tdot(q, k)
                for j in range(rep):
                    chains.append(dict(c=c, rows=rows, ch=a * rep + j, hv=(hq0 + a) * rep + j,
                                       gtile=gtile, btile=btile, q=q, k=k, kk_w=kk_w, qk_t=qk_t))
        for s in chains:
            s['gcol'] = jnp.sum(jnp.where(lane == s['hv'], s['gtile'], 0.0), axis=1, keepdims=True)
            s['bcol'] = jnp.sum(jnp.where(lane == s['hv'], s['btile'], 0.0), axis=1, keepdims=True)
            grow = gcr_ref[0, s['ch'], pl.ds(s['c'], 1), :]
            s['decay'] = jnp.exp(jnp.where(causal, s['gcol'] - grow, -jnp.inf))
            pt = jnp.where(strict & left, -(s['kk_w'] * s['bcol'] * s['decay']), 0.0)
            s['pt'] = pt + eye_hi
        for _ in range(6):
            for s in chains:
                s['ptb'] = s['pt'].astype(BF16)
            for s in chains:
                s['x'] = dot(s['ptb'][:, :c_len], s['ptb'])
            for s in chains:
                s['pt'] = jnp.where(left, s['x'], s['pt'] + s['x'])
        for s in chains:
            tmat = pltpu.roll(s['pt'], c_len, 1)[:, :c_len].astype(BF16)
            s['egc'] = jnp.exp(s['gcol'])
            s['kf'] = s['k'].astype(F32)
            vf = v_ref[0, s['rows'], s['ch'] * d:(s['ch'] + 1) * d].astype(F32)
            rhs = jnp.concatenate([vf * s['bcol'], s['kf'] * (s['bcol'] * s['egc'])], axis=1).astype(BF16)
            s['sol'] = dot(tmat, rhs)
        for s in chains:
            sol = s['sol'].astype(BF16)
            glast = s['gcol'][c_len - 1:c_len, :]
            kd = (s['kf'] * jnp.exp(glast - s['gcol'])).astype(BF16)
            aqk = (s['qk_t'] * s['decay'][:, :c_len]).astype(BF16)
            s['big'] = lax.dot_general(kd, sol, (((0,), (0,)), ((), ())), preferred_element_type=F32)
            s['small'] = dot(aqk, sol)
            s['cd'] = jnp.exp(glast)
        for s in chains:
            ch, c, rows = s['ch'], s['c'], s['rows']
            b_ref[ch, c] = s['big'][:, :d]
            m_ref[ch, c] = s['big'][:, d:].astype(BF16)
            oacc_ref[ch, rows, :] = s['small'][:, :d]
            qe_ref[ch, rows, :] = (s['q'].astype(F32) * s['egc'] - s['small'][:, d:]).astype(BF16)
            cd_ref[ch, pl.ds(c, 1), :] = jnp.broadcast_to(s['cd'], (1, LANES))
        return carry

    lax.fori_loop(0, n_chunks // unroll, prepare, 0)

    def recur(c, carry):
        rows = pl.ds(pl.multiple_of(c * c_len, c_len), c_len)
        for ch in range(hpg * rep):
            st = state_ref[ch]
            st_b = st.astype(BF16)
            oacc_ref[ch, rows, :] = oacc_ref[ch, rows, :] + dot(qe_ref[ch, rows, :], st_b)
            state_ref[ch] = st * cd_ref[ch, pl.ds(c, 1), :] - dot(m_ref[ch, c], st_b) + b_ref[ch, c]
        return carry

    lax.fori_loop(0, n_chunks, recur, 0)

    nw = nw_ref[...]
    for ch in range(hpg * rep):
        o = oacc_ref[ch]
        on = o * lax.rsqrt(jnp.mean(o * o, axis=-1, keepdims=True) + HEAD_NORM_EPS) * nw
        zf = z_ref[0, :, ch * d:(ch + 1) * d].astype(F32)
        o_ref[0, :, ch * d:(ch + 1) * d] = (on * (zf * jax.nn.sigmoid(zf))).astype(BF16)


def gdn2(qkv, p_a, gc, beta, gc_rows, norm_w, *, tb=512, hpg=2, unroll=4):
    b, s, _ = qkv.shape
    tb = min(tb, s)
    assert s % tb == 0 and tb % (LA_CHUNK * unroll) == 0 and LA_QK_HEADS % hpg == 0
    rep = LA_V_HEADS // LA_QK_HEADS
    d = LA_HEAD_DIM
    qw = hpg * d
    vw = hpg * rep * d
    nch = hpg * rep
    n_chunks = tb // LA_CHUNK
    return pl.pallas_call(
        functools.partial(_gdn2_kernel, tb=tb, hpg=hpg, rep=rep, unroll=unroll),
        grid=(b, LA_QK_HEADS // hpg, s // tb),
        in_specs=[
            pl.BlockSpec((1, tb, qw), lambda bi, h, si: (bi, si, h)),
            pl.BlockSpec((1, tb, qw), lambda bi, h, si: (bi, si, LA_QK_DIM // qw + h)),
            pl.BlockSpec((1, tb, vw), lambda bi, h, si: (bi, si, 2 * LA_QK_DIM // vw + h)),
            pl.BlockSpec((1, tb, vw), lambda bi, h, si: (bi, si, LA_CONV_DIM // vw + h)),
            pl.BlockSpec((1, tb, LANES), lambda bi, h, si: (bi, si, 0)),
            pl.BlockSpec((1, tb, LANES), lambda bi, h, si: (bi, si, 0)),
            pl.BlockSpec((1, nch, n_chunks, 2 * LA_CHUNK), lambda bi, h, si: (bi, h, si, 0)),
            pl.BlockSpec((1, d), lambda bi, h, si: (0, 0)),
        ],
        out_specs=pl.BlockSpec((1, tb, vw), lambda bi, h, si: (bi, si, h)),
        out_shape=jax.ShapeDtypeStruct((b, s, LA_V_DIM), BF16),
        scratch_shapes=[
            pltpu.VMEM((nch, d, d), F32),
            pltpu.VMEM((nch, n_chunks, d, d), BF16),
            pltpu.VMEM((nch, n_chunks, d, d), F32),
            pltpu.VMEM((nch, tb, d), BF16),
            pltpu.VMEM((nch, tb, d), F32),
            pltpu.VMEM((nch, n_chunks, LANES), F32),
        ],
        compiler_params=_cparams(("parallel", "parallel", "arbitrary")),
        name="gdn",
    )(qkv, qkv, qkv, p_a, gc, beta, gc_rows, norm_w.reshape(1, d).astype(F32))


def _head_rms(x, w):
    return x * lax.rsqrt(jnp.mean(x * x, axis=-1, keepdims=True) + HEAD_NORM_EPS) * w


def _swa_kernel(sink_ref, q_ref, kp_ref, kc_ref, vp_ref, vc_ref, qw_ref, kw_ref, o_ref):
    n = pl.program_id(1)
    blk = WINDOW
    dh = SWA_HEAD_DIM
    qi = lax.broadcasted_iota(I32, (blk, 2 * blk), 0)
    kj = lax.broadcasted_iota(I32, (blk, 2 * blk), 1)
    dist = qi + blk - kj
    valid = (dist >= 0) & (dist < WINDOW) & ((n > 0) | (kj >= blk))
    dist_f = dist.astype(F32)
    qw = qw_ref[...]
    kw = kw_ref[...]
    for g in range(SWA_KV_HEADS):
        sl = slice(g * dh, (g + 1) * dh)
        kb = jnp.concatenate([kp_ref[0, :, sl], kc_ref[0, :, sl]], axis=0).astype(F32)
        kb = _head_rms(kb, kw).astype(BF16)
        vb = jnp.concatenate([vp_ref[0, :, sl], vc_ref[0, :, sl]], axis=0)
        qs = []
        for j in range(SWA_GROUP):
            h = g * SWA_GROUP + j
            qh = q_ref[0, :, h * dh:(h + 1) * dh].astype(F32)
            qs.append(_head_rms(qh, qw).astype(BF16))
        qg = jnp.concatenate(qs, axis=0)
        sc = lax.dot_general(qg, kb, (((1,), (1,)), ((), ())), preferred_element_type=F32)
        ps = []
        for j in range(SWA_GROUP):
            h = g * SWA_GROUP + j
            slope = 2.0 ** (-8.0 * (h + 1) / SWA_Q_HEADS)
            sink = sink_ref[h]
            s_h = sc[j * blk:(j + 1) * blk, :] * (dh ** -0.5) - slope * dist_f
            s_h = jnp.where(valid, s_h, -jnp.inf)
            m = jnp.maximum(jnp.max(s_h, axis=-1, keepdims=True), sink)
            p = jnp.exp(s_h - m)
            denom = jnp.sum(p, axis=-1, keepdims=True) + jnp.exp(sink - m)
            ps.append((p / denom).astype(BF16))
        pg = jnp.concatenate(ps, axis=0)
        og = jnp.dot(pg, vb, preferred_element_type=F32)
        for j in range(SWA_GROUP):
            h = g * SWA_GROUP + j
            o_ref[0, :, h * dh:(h + 1) * dh] = og[j * blk:(j + 1) * blk, :].astype(BF16)


def swa(p_b, sinks, q_norm_w, k_norm_w):
    b, s, _ = p_b.shape
    blk = WINDOW
    assert s % blk == 0
    kblk = SWA_Q_DIM // SWA_KV_DIM
    prev = lambda bi, n: jnp.maximum(n - 1, 0)
    return pl.pallas_call(
        _swa_kernel,
        grid=(b, s // blk),
        in_specs=[
            pl.BlockSpec(memory_space=pltpu.SMEM),
            pl.BlockSpec((1, blk, SWA_Q_DIM), lambda bi, n: (bi, n, 0)),
            pl.BlockSpec((1, blk, SWA_KV_DIM), lambda bi, n: (bi, prev(bi, n), kblk)),
            pl.BlockSpec((1, blk, SWA_KV_DIM), lambda bi, n: (bi, n, kblk)),
            pl.BlockSpec((1, blk, SWA_KV_DIM), lambda bi, n: (bi, prev(bi, n), kblk + 1)),
            pl.BlockSpec((1, blk, SWA_KV_DIM), lambda bi, n: (bi, n, kblk + 1)),
            pl.BlockSpec((1, SWA_HEAD_DIM), lambda bi, n: (0, 0)),
            pl.BlockSpec((1, SWA_HEAD_DIM), lambda bi, n: (0, 0)),
        ],
        out_specs=pl.BlockSpec((1, blk, SWA_Q_DIM), lambda bi, n: (bi, n, 0)),
        out_shape=jax.ShapeDtypeStruct((b, s, SWA_Q_DIM), BF16),
        compiler_params=_cparams(("parallel", "parallel")),
        name="swa",
    )(sinks.astype(F32), p_b, p_b, p_b, p_b, p_b,
      q_norm_w.reshape(1, SWA_HEAD_DIM).astype(F32), k_norm_w.reshape(1, SWA_HEAD_DIM).astype(F32))


def _merge_kernel(a_ref, wla_ref, bm_ref, wsw_ref, gla_ref, gsw_ref, bla_ref, bsw_ref, o_ref):
    oa = jnp.dot(a_ref[...], wla_ref[...], preferred_element_type=F32)
    ob = jnp.dot(bm_ref[...], wsw_ref[...], preferred_element_type=F32)
    gl = jax.nn.sigmoid(gla_ref[...].astype(F32) + bla_ref[...])
    gs = jax.nn.sigmoid(gsw_ref[...].astype(F32) + bsw_ref[...])
    o_ref[...] = (gl * oa + gs * ob).astype(BF16)


def merge(o_a, w_la, o_b, w_sw, p_b, b_gate, *, tm=1024, tn=512):
    m, ka = o_a.shape
    kb = o_b.shape[1]
    dm = w_la.shape[1]
    tm = min(tm, m)
    assert m % tm == 0 and dm % tn == 0
    gate0 = (SWA_Q_DIM + 2 * SWA_KV_DIM) // tn
    assert gate0 * tn == SWA_Q_DIM + 2 * SWA_KV_DIM
    nblk = dm // tn
    bg = b_gate.astype(F32).reshape(1, 2 * dm)
    return pl.pallas_call(
        _merge_kernel,
        grid=(m // tm, nblk),
        in_specs=[
            pl.BlockSpec((tm, ka), lambda i, j: (i, 0)),
            pl.BlockSpec((ka, tn), lambda i, j: (0, j)),
            pl.BlockSpec((tm, kb), lambda i, j: (i, 0)),
            pl.BlockSpec((kb, tn), lambda i, j: (0, j)),
            pl.BlockSpec((tm, tn), lambda i, j: (i, gate0 + j)),
            pl.BlockSpec((tm, tn), lambda i, j: (i, gate0 + nblk + j)),
            pl.BlockSpec((1, tn), lambda i, j: (0, j)),
            pl.BlockSpec((1, tn), lambda i, j: (0, nblk + j)),
        ],
        out_specs=pl.BlockSpec((tm, tn), lambda i, j: (i, j)),
        out_shape=jax.ShapeDtypeStruct((m, dm), BF16),
        compiler_params=_cparams(("parallel", "arbitrary")),
        name="merge",
    )(o_a, w_la, o_b, w_sw, p_b, p_b, bg, bg)


U32 = jnp.uint32
HIGH_HALF = 0xFFFF0000


def _pack_rows(x):
    bits = lax.bitcast_convert_type(x.astype(BF16).astype(F32), U32)
    half = x.shape[1] // 2
    return (bits[:, :half] >> U32(16)) | (bits[:, half:] & U32(HIGH_HALF))


def _unpack_rows(w):
    lo = lax.bitcast_convert_type(w << U32(16), F32)
    hi = lax.bitcast_convert_type(w & U32(HIGH_HALF), F32)
    return lo, hi


def _oproj_router_kernel(x_ref, m_ref, wo_ref, g_ref, wr_ref, br_ref,
                         x1_ref, h2_ref, idx_ref, wgt_ref, rank_ref, cnt_ref, carry_ref, *, tm):
    @pl.when(pl.program_id(0) == 0)
    def _():
        carry_ref[...] = jnp.zeros_like(carry_ref)

    x1 = x_ref[...] + jnp.dot(m_ref[...], wo_ref[...], preferred_element_type=F32)
    x1_ref[...] = x1
    var = jnp.mean(x1 * x1, axis=-1, keepdims=True)
    h2 = x1 * lax.rsqrt(var + NORM_EPS) * g_ref[...]
    h2_ref[...] = _pack_rows(h2)
    lane = lax.broadcasted_iota(I32, (tm, LANES), 1)
    logits = jnp.dot(h2.astype(BF16), wr_ref[...], preferred_element_type=F32) + br_ref[...]
    logits = jnp.where(lane < N_EXPERTS, logits, -jnp.inf)
    vals, idxs = [], []
    for _ in range(TOP_K):
        mx = jnp.max(logits, axis=-1, keepdims=True)
        ix = jnp.min(jnp.where(logits == mx, lane, LANES), axis=-1, keepdims=True)
        vals.append(mx)
        idxs.append(ix)
        logits = jnp.where(lane == ix, -jnp.inf, logits)
    exps = [jnp.exp(v - vals[0]) for v in vals]
    tot = exps[0]
    for e in exps[1:]:
        tot = tot + e
    multihot = jnp.zeros((tm, LANES), F32)
    for ix in idxs:
        multihot = multihot + jnp.where(lane == ix, 1.0, 0.0)
    row = lax.broadcasted_iota(I32, (tm, tm), 0)
    col = lax.broadcasted_iota(I32, (tm, tm), 1)
    lower = jnp.where(col < row, 1.0, 0.0).astype(BF16)
    before = jnp.dot(lower, multihot.astype(BF16), preferred_element_type=F32) + carry_ref[...]
    idx_o = jnp.zeros((tm, LANES), I32)
    wgt_o = jnp.zeros((tm, LANES), F32)
    rank_o = jnp.zeros((tm, LANES), I32)
    for kk in range(TOP_K):
        rk = jnp.sum(jnp.where(lane == idxs[kk], before, 0.0), axis=-1, keepdims=True)
        idx_o = jnp.where(lane == kk, idxs[kk], idx_o)
        wgt_o = jnp.where(lane == kk, exps[kk] / tot, wgt_o)
        rank_o = jnp.where(lane == kk, rk.astype(I32), rank_o)
    idx_ref[...] = idx_o
    wgt_ref[...] = wgt_o
    rank_ref[...] = rank_o
    carry = carry_ref[...] + jnp.sum(multihot, axis=0, keepdims=True)
    carry_ref[...] = carry
    cnt_ref[...] = carry.astype(I32)


def oproj_router(x, m_act, w_o, gain, w_router, b_router, *, tm=256):
    m, dm = x.shape
    tm = min(tm, m)
    assert m % tm == 0
    wr = jnp.pad(w_router.astype(BF16), ((0, 0), (0, LANES - N_EXPERTS)))
    br = jnp.pad(b_router.astype(F32), (0, LANES - N_EXPERTS)).reshape(1, LANES)
    tok = lambda i: (i, 0)
    fixed = lambda i: (0, 0)
    return pl.pallas_call(
        functools.partial(_oproj_router_kernel, tm=tm),
        grid=(m // tm,),
        in_specs=[
            pl.BlockSpec((tm, dm), tok),
            pl.BlockSpec((tm, dm), tok),
            pl.BlockSpec((dm, dm), fixed),
            pl.BlockSpec((1, dm), fixed),
            pl.BlockSpec((dm, LANES), fixed),
            pl.BlockSpec((1, LANES), fixed),
        ],
        out_specs=[
            pl.BlockSpec((tm, dm), tok),
            pl.BlockSpec((tm, dm // 2), tok),
            pl.BlockSpec((tm, LANES), tok),
            pl.BlockSpec((tm, LANES), tok),
            pl.BlockSpec((tm, LANES), tok),
            pl.BlockSpec((1, LANES), fixed),
        ],
        out_shape=[
            jax.ShapeDtypeStruct((m, dm), F32),
            jax.ShapeDtypeStruct((m, dm // 2), U32),
            jax.ShapeDtypeStruct((m, LANES), I32),
            jax.ShapeDtypeStruct((m, LANES), F32),
            jax.ShapeDtypeStruct((m, LANES), I32),
            jax.ShapeDtypeStruct((1, LANES), I32),
        ],
        scratch_shapes=[pltpu.VMEM((1, LANES), F32)],
        compiler_params=_cparams(("arbitrary",)),
        name="oproj_router",
    )(x, m_act, w_o, gain.reshape(1, dm).astype(F32), wr, br)


def _row_copy(src, dst, sem, src_row, dst_row):
    return pltpu.make_async_copy(src.at[pl.ds(src_row, 1), :], dst.at[pl.ds(dst_row, 1), :], sem)


def _dispatch_kernel(pos_ref, h_ref, zero_hbm, xs_hbm, sem, *, td):
    del zero_hbm

    def issue(i, carry):
        for kk in range(TOP_K):
            _row_copy(h_ref, xs_hbm, sem, i, pos_ref[i * TOP_K + kk]).start()
        return carry

    lax.fori_loop(0, td, issue, 0, unroll=4)

    def drain(i, carry):
        _row_copy(h_ref, xs_hbm, sem, 0, 0).wait()
        return carry

    lax.fori_loop(0, td * TOP_K, drain, 0, unroll=8)


def dispatch(h_rows, pos_flat, n_rows, *, td=256):
    m, width = h_rows.shape
    td = min(td, m)
    assert m % td == 0
    zeros = jnp.zeros((n_rows, width), h_rows.dtype)
    return pl.pallas_call(
        functools.partial(_dispatch_kernel, td=td),
        grid=(m // td,),
        in_specs=[
            pl.BlockSpec((td * TOP_K,), lambda i: (i,), memory_space=pltpu.SMEM),
            pl.BlockSpec((td, width), lambda i: (i, 0)),
            pl.BlockSpec(memory_space=pl.ANY),
        ],
        out_specs=pl.BlockSpec(memory_space=pl.ANY),
        out_shape=jax.ShapeDtypeStruct(zeros.shape, zeros.dtype),
        scratch_shapes=[pltpu.SemaphoreType.DMA(())],
        input_output_aliases={2: 0},
        compiler_params=_cparams(("arbitrary",)),
        name="dispatch",
    )(pos_flat, h_rows, zeros)


MXU_TILE = 256


def _deinterleave_kernel(w_ref, g_ref, l_ref):
    half = MXU_TILE // 2
    r = lax.broadcasted_iota(I32, (MXU_TILE, MXU_TILE), 0)
    c = lax.broadcasted_iota(I32, (MXU_TILE, MXU_TILE), 1)
    perm = jnp.where(r == jnp.where(c < half, 2 * c, 2 * (c - half) + 1), 1.0, 0.0).astype(BF16)
    for blk in range(w_ref.shape[2] // MXU_TILE):
        piece = w_ref[0, :, blk * MXU_TILE:(blk + 1) * MXU_TILE].astype(BF16)
        out = jnp.dot(piece, perm, preferred_element_type=F32)
        g_ref[0, :, blk * half:(blk + 1) * half] = out[:, :half].astype(BF16)
        l_ref[0, :, blk * half:(blk + 1) * half] = out[:, half:].astype(BF16)


def deinterleave(w, *, tk=512, tn=1024):
    e, k, n2 = w.shape
    n = n2 // 2
    assert k % tk == 0 and n % tn == 0 and (2 * tn) % MXU_TILE == 0
    out = jax.ShapeDtypeStruct((e, k, n), BF16)
    return pl.pallas_call(
        _deinterleave_kernel,
        grid=(e, k // tk, n // tn),
        in_specs=[pl.BlockSpec((1, tk, 2 * tn), lambda ei, ki, ni: (ei, ki, ni))],
        out_specs=[pl.BlockSpec((1, tk, tn), lambda ei, ki, ni: (ei, ki, ni)),
                   pl.BlockSpec((1, tk, tn), lambda ei, ki, ni: (ei, ki, ni))],
        out_shape=[out, out],
        compiler_params=_cparams(("parallel", "parallel", "parallel")),
        name="deinterleave",
    )(w)


def _experts_kernel(be_ref, nv_ref, x_ref, w1g_ref, w1l_ref, b1g_ref, b1l_ref, w2_ref, b2_ref,
                    o_ref, acc_ref, xb_ref):
    i = pl.program_id(0)
    j = pl.program_id(1)
    nj = pl.num_programs(1)
    live = i < nv_ref[0]
    half = x_ref.shape[1]

    @pl.when(live & (j == 0))
    def _():
        lo, hi = _unpack_rows(x_ref[...])
        xb_ref[:, :half] = lo.astype(BF16)
        xb_ref[:, half:] = hi.astype(BF16)

    @pl.when(live)
    def _():
        x = xb_ref[...]
        hg = jnp.dot(x, w1g_ref[0], preferred_element_type=F32) + b1g_ref[0]
        hl = jnp.dot(x, w1l_ref[0], preferred_element_type=F32) + b1l_ref[0]
        glu = jnp.minimum(hg, SWIGLU_LIMIT)
        lin = jnp.clip(hl, -SWIGLU_LIMIT, SWIGLU_LIMIT)
        act = glu * jax.nn.sigmoid(SWIGLU_ALPHA * glu) * (lin + 1.0)
        part = jnp.dot(act.astype(BF16), w2_ref[0], preferred_element_type=F32)

        @pl.when(j == 0)
        def _():
            acc_ref[...] = part + b2_ref[0]

        @pl.when(j > 0)
        def _():
            acc_ref[...] = acc_ref[...] + part

        @pl.when(j == nj - 1)
        def _():
            o_ref[...] = _pack_rows(acc_ref[...])

    @pl.when(jnp.logical_not(live) & (j == nj - 1))
    def _():
        o_ref[...] = jnp.zeros_like(o_ref)


def experts(xs, blk_expert, n_live, w1g, w1l, b1g, b1l, w2, b2, *, tj=1024):
    n_rows, half = xs.shape
    dm = 2 * half
    n_blk = n_rows // MOE_ROWS
    de = w2.shape[1]
    assert de % tj == 0
    nj = de // tj

    def wsel(i, j, be, nv):
        return be[i], jnp.where(i < nv[0], j, nj - 1)

    grid_spec = pltpu.PrefetchScalarGridSpec(
        num_scalar_prefetch=2,
        grid=(n_blk, nj),
        in_specs=[
            pl.BlockSpec((MOE_ROWS, half), lambda i, j, be, nv: (jnp.clip(i, 0, jnp.maximum(nv[0] - 1, 0)), 0)),
            pl.BlockSpec((1, dm, tj), lambda i, j, be, nv: (wsel(i, j, be, nv)[0], 0, wsel(i, j, be, nv)[1])),
            pl.BlockSpec((1, dm, tj), lambda i, j, be, nv: (wsel(i, j, be, nv)[0], 0, wsel(i, j, be, nv)[1])),
            pl.BlockSpec((1, 1, tj), lambda i, j, be, nv: (wsel(i, j, be, nv)[0], 0, wsel(i, j, be, nv)[1])),
            pl.BlockSpec((1, 1, tj), lambda i, j, be, nv: (wsel(i, j, be, nv)[0], 0, wsel(i, j, be, nv)[1])),
            pl.BlockSpec((1, tj, dm), lambda i, j, be, nv: (wsel(i, j, be, nv)[0], wsel(i, j, be, nv)[1], 0)),
            pl.BlockSpec((1, 1, dm), lambda i, j, be, nv: (be[i], 0, 0)),
        ],
        out_specs=pl.BlockSpec((MOE_ROWS, half), lambda i, j, be, nv: (i, 0)),
        scratch_shapes=[pltpu.VMEM((MOE_ROWS, dm), F32), pltpu.VMEM((MOE_ROWS, dm), BF16)],
    )
    return pl.pallas_call(
        _experts_kernel,
        grid_spec=grid_spec,
        out_shape=jax.ShapeDtypeStruct((n_rows, half), U32),
        compiler_params=_cparams(("arbitrary", "arbitrary")),
        name="experts",
    )(blk_expert, n_live, xs, w1g, w1l, b1g, b1l, w2, b2)


def _combine_kernel(pos_ref, pos_next_ref, x1_ref, wgt_ref, y_hbm, o_ref, ybuf_ref, sem, *, tt):
    i = pl.program_id(0)
    slot = lax.rem(i, 2)
    half = ybuf_ref.shape[-1]

    def gather(p_ref, sl):
        def issue(t, carry):
            for kk in range(TOP_K):
                _row_copy(y_hbm, ybuf_ref.at[sl, kk], sem.at[sl], p_ref[t * TOP_K + kk], t).start()
            return carry

        lax.fori_loop(0, tt, issue, 0, unroll=4)

    @pl.when(i == 0)
    def _():
        gather(pos_ref, 0)

    @pl.when(i + 1 < pl.num_programs(0))
    def _():
        gather(pos_next_ref, 1 - slot)

    def drain(r, carry):
        _row_copy(y_hbm, ybuf_ref.at[slot, 0], sem.at[slot], 0, 0).wait()
        return carry

    lax.fori_loop(0, tt * TOP_K, drain, 0, unroll=8)

    w = wgt_ref[...]
    acc_lo = x1_ref[:, :half]
    acc_hi = x1_ref[:, half:]
    for kk in range(TOP_K):
        lo, hi = _unpack_rows(ybuf_ref[slot, kk])
        wk = w[:, kk:kk + 1]
        acc_lo = acc_lo + wk * lo
        acc_hi = acc_hi + wk * hi
    o_ref[:, :half] = acc_lo
    o_ref[:, half:] = acc_hi


def combine(x1, y_rows, pos_flat, wgt, *, tt=256):
    m, dm = x1.shape
    half = y_rows.shape[1]
    tt = min(tt, m)
    assert m % tt == 0 and dm == 2 * half
    n_tiles = m // tt
    return pl.pallas_call(
        functools.partial(_combine_kernel, tt=tt),
        grid=(n_tiles,),
        in_specs=[
            pl.BlockSpec((tt * TOP_K,), lambda i: (i,), memory_space=pltpu.SMEM),
            pl.BlockSpec((tt * TOP_K,), lambda i: (jnp.minimum(i + 1, n_tiles - 1),), memory_space=pltpu.SMEM),
            pl.BlockSpec((tt, dm), lambda i: (i, 0)),
            pl.BlockSpec((tt, LANES), lambda i: (i, 0)),
            pl.BlockSpec(memory_space=pl.ANY),
        ],
        out_specs=pl.BlockSpec((tt, dm), lambda i: (i, 0)),
        out_shape=jax.ShapeDtypeStruct((m, dm), F32),
        scratch_shapes=[pltpu.VMEM((2, TOP_K, tt, half), y_rows.dtype), pltpu.SemaphoreType.DMA((2,))],
        compiler_params=_cparams(("arbitrary",)),
        name="combine",
    )(pos_flat, pos_flat, x1, wgt, y_rows)


def _layer(x, ln_mix_w, w_in, b_gate, conv_w, a_log, dt_bias, la_norm_w, w_out_la, q_norm_w,
           k_norm_w, sinks, w_out_swa, w_o, ln_ffn_w, w_router, b_router, w1, b1, w2, b2):
    bsz, seq, dm = x.shape
    n_tok = bsz * seq
    xf = x.reshape(n_tok, dm)

    c_z_end = LA_CONV_DIM + LA_V_DIM
    c_ba_end = c_z_end + 2 * LA_V_HEADS
    w_a = w_in[:, :c_z_end].astype(BF16)
    w_b = w_in[:, c_ba_end:].astype(BF16)
    lane_pad = ((0, 0), (0, LANES - LA_V_HEADS))
    w_c = jnp.concatenate([jnp.pad(w_in[:, c_z_end:c_z_end + LA_V_HEADS], lane_pad),
                           jnp.pad(w_in[:, c_z_end + LA_V_HEADS:c_ba_end], lane_pad)], axis=1).astype(BF16)

    p_a = rms_matmul(xf, ln_mix_w, w_a, out_dtype=BF16, tm=1024, tn=1024)
    p_b = rms_matmul(xf, ln_mix_w, w_b, out_dtype=BF16, tm=1024, tn=512)
    p_c = rms_matmul(xf, ln_mix_w, w_c, out_dtype=F32, tm=1024, tn=2 * LANES)

    p_a3 = p_a.reshape(bsz, seq, c_z_end)
    qkv = conv_prep(p_a3, conv_w)
    beta, gc = gates(p_c, a_log, dt_bias)
    gc_rows = gc[:, :LA_V_HEADS].reshape(bsz, seq // LA_CHUNK, LA_CHUNK, LA_V_HEADS).transpose(0, 3, 1, 2)
    gc_rows = jnp.concatenate([gc_rows, gc_rows], axis=-1)
    o_a = gdn2(qkv, p_a3, gc.reshape(bsz, seq, LANES), beta.reshape(bsz, seq, LANES), gc_rows, la_norm_w)

    p_b3 = p_b.reshape(bsz, seq, p_b.shape[1])
    o_b = swa(p_b3, sinks, q_norm_w, k_norm_w)

    m_act = merge(o_a.reshape(n_tok, LA_V_DIM), w_out_la.astype(BF16),
                  o_b.reshape(n_tok, SWA_Q_DIM), w_out_swa.astype(BF16), p_b, b_gate)
    x1, h2, idx, wgt, rank, counts = oproj_router(xf, m_act, w_o.astype(BF16), ln_ffn_w, w_router, b_router)

    counts = counts[0, :N_EXPERTS]
    padded = (counts + MOE_ROWS - 1) // MOE_ROWS * MOE_ROWS
    pend = jnp.cumsum(padded)
    pstart = pend - padded
    n_blk = n_tok * TOP_K // MOE_ROWS + N_EXPERTS
    n_rows = n_blk * MOE_ROWS
    n_live = (pend[-1:] // MOE_ROWS).astype(I32)
    blk_start = jnp.minimum(jnp.arange(n_blk, dtype=I32), jnp.maximum(n_live[0] - 1, 0)) * MOE_ROWS
    blk_expert = jnp.minimum(jnp.sum(pend[None, :] <= blk_start[:, None], axis=1), N_EXPERTS - 1).astype(I32)
    pos_flat = (pstart[idx[:, :TOP_K]] + rank[:, :TOP_K]).reshape(n_tok * TOP_K).astype(I32)

    xs = dispatch(h2, pos_flat, n_rows)
    de = w2.shape[1]
    w1g, w1l = deinterleave(w1)
    y = experts(xs, blk_expert, n_live, w1g, w1l,
                b1[:, 0::2].reshape(N_EXPERTS, 1, de).astype(F32), b1[:, 1::2].reshape(N_EXPERTS, 1, de).astype(F32),
                w2.astype(BF16), b2.reshape(N_EXPERTS, 1, dm).astype(F32))
    out = combine(x1, y, pos_flat, wgt)
    return out.reshape(bsz, seq, dm)


def kernel(x, ln_mix_w, w_in, b_gate, conv_w, a_log, dt_bias, la_norm_w, w_out_la, q_norm_w, k_norm_w,
           sinks, w_out_swa, w_o, ln_ffn_w, w_router, b_router, w1, b1, w2, b2):
    params = (ln_mix_w, w_in, b_gate, conv_w, a_log, dt_bias, la_norm_w, w_out_la, q_norm_w, k_norm_w,
              sinks, w_out_swa, w_o, ln_ffn_w, w_router, b_router, w1, b1, w2, b2)
    for layer in range(ln_mix_w.shape[0]):
        x = _layer(x, *(p[layer] for p in params))
    return x
```

```python
import functools

import jax
import jax.numpy as jnp
from jax import lax
from jax.experimental import pallas as pl
from jax.experimental.pallas import tpu as pltpu

F32 = jnp.float32
BF16 = jnp.bfloat16
I32 = jnp.int32

LA_QK_HEADS = 16
LA_V_HEADS = 32
LA_HEAD_DIM = 128
LA_CONV = 4
LA_CHUNK = 64
LA_QK_DIM = LA_QK_HEADS * LA_HEAD_DIM
LA_V_DIM = LA_V_HEADS * LA_HEAD_DIM
LA_CONV_DIM = 2 * LA_QK_DIM + LA_V_DIM
SWA_Q_HEADS = 32
SWA_KV_HEADS = 4
SWA_GROUP = SWA_Q_HEADS // SWA_KV_HEADS
SWA_HEAD_DIM = 64
SWA_Q_DIM = SWA_Q_HEADS * SWA_HEAD_DIM
SWA_KV_DIM = SWA_KV_HEADS * SWA_HEAD_DIM
WINDOW = 128
N_EXPERTS = 32
TOP_K = 4
SWIGLU_LIMIT = 7.0
SWIGLU_ALPHA = 1.702
NORM_EPS = 1e-5
HEAD_NORM_EPS = 1e-6
L2_EPS = 1e-6

U32 = jnp.uint32

LANES = 128
ROW_TILE = 16
MXU_TILE = 256
HIGH_HALF = 0xFFFF0000
MOE_ROWS = 512
VMEM_LIMIT = 56 * 1024 * 1024


def _cparams(sem, vmem=VMEM_LIMIT):
    return pltpu.CompilerParams(dimension_semantics=sem, vmem_limit_bytes=vmem)


def _rms_matmul_kernel(x_ref, g_ref, w_ref, o_ref, h_ref, *, eps, rows):
    @pl.when(pl.program_id(1) == 0)
    def _():
        for r in range(0, x_ref.shape[0], rows):
            x = x_ref[r:r + rows, :]
            var = jnp.mean(x * x, axis=-1, keepdims=True)
            h_ref[r:r + rows, :] = (x * lax.rsqrt(var + eps) * g_ref[...]).astype(BF16)

    o_ref[...] = jnp.dot(h_ref[...], w_ref[...], preferred_element_type=F32).astype(o_ref.dtype)


def rms_matmul(x, gain, w, *, out_dtype, tm, tn):
    m, k = x.shape
    n = w.shape[1]
    tm = min(tm, m)
    tn = min(tn, n)
    assert m % tm == 0 and n % tn == 0
    return pl.pallas_call(
        functools.partial(_rms_matmul_kernel, eps=NORM_EPS, rows=min(256, tm)),
        grid=(m // tm, n // tn),
        in_specs=[
            pl.BlockSpec((tm, k), lambda i, j: (i, 0)),
            pl.BlockSpec((1, k), lambda i, j: (0, 0)),
            pl.BlockSpec((k, tn), lambda i, j: (0, j)),
        ],
        out_specs=pl.BlockSpec((tm, tn), lambda i, j: (i, j)),
        out_shape=jax.ShapeDtypeStruct((m, n), out_dtype),
        scratch_shapes=[pltpu.VMEM((tm, k), BF16)],
        compiler_params=_cparams(("parallel", "arbitrary")),
        name="rms_matmul",
    )(x, gain.reshape(1, k), w)


def _conv_kernel(x_ref, halo_ref, w_ref, o_ref, buf_ref, *, tc, cw, n_q_tiles, n_qk_tiles):
    s = pl.program_id(1)
    c = pl.program_id(2)
    halo = halo_ref[0].astype(F32)
    buf_ref[0:ROW_TILE, :] = jnp.where(s > 0, halo, 0.0)
    buf_ref[ROW_TILE:ROW_TILE + tc, :] = x_ref[0].astype(F32)
    acc = jnp.zeros((tc, cw), F32)
    for j in range(LA_CONV):
        acc = acc + w_ref[j:j + 1, :] * buf_ref[pl.ds(ROW_TILE - (LA_CONV - 1) + j, tc), :]
    y = acc * jax.nn.sigmoid(acc)
    q_scale = jnp.where(c < n_q_tiles, LA_HEAD_DIM ** -0.5, 1.0)
    for hh in range(cw // LA_HEAD_DIM):
        yh = y[:, hh * LA_HEAD_DIM:(hh + 1) * LA_HEAD_DIM]
        ss = jnp.sum(yh * yh, axis=-1, keepdims=True)
        f = jnp.where(c < n_qk_tiles, lax.rsqrt(ss + L2_EPS), 1.0) * q_scale
        o_ref[0, :, hh * LA_HEAD_DIM:(hh + 1) * LA_HEAD_DIM] = (yh * f).astype(BF16)


def conv_prep(p_a, conv_w, *, tc=512, cw=512):
    b, s, _ = p_a.shape
    tc = min(tc, s)
    assert s % tc == 0 and tc % ROW_TILE == 0 and LA_QK_DIM % cw == 0
    halo_blocks = tc // ROW_TILE
    return pl.pallas_call(
        functools.partial(_conv_kernel, tc=tc, cw=cw, n_q_tiles=LA_QK_DIM // cw,
                          n_qk_tiles=2 * LA_QK_DIM // cw),
        grid=(b, s // tc, LA_CONV_DIM // cw),
        in_specs=[
            pl.BlockSpec((1, tc, cw), lambda bi, si, ci: (bi, si, ci)),
            pl.BlockSpec((1, ROW_TILE, cw),
                         lambda bi, si, ci: (bi, jnp.maximum(si * halo_blocks - 1, 0), ci)),
            pl.BlockSpec((LA_CONV, cw), lambda bi, si, ci: (0, ci)),
        ],
        out_specs=pl.BlockSpec((1, tc, cw), lambda bi, si, ci: (bi, si, ci)),
        out_shape=jax.ShapeDtypeStruct((b, s, LA_CONV_DIM), BF16),
        scratch_shapes=[pltpu.VMEM((ROW_TILE + tc, cw), F32)],
        compiler_params=_cparams(("parallel", "parallel", "parallel")),
        name="conv_prep",
    )(p_a, p_a, conv_w)


def _gates_kernel(p_ref, alog_ref, dtb_ref, beta_ref, gc_ref, *, tg):
    bproj = p_ref[:, 0:LANES]
    aproj = p_ref[:, LANES:2 * LANES]
    beta_ref[...] = jax.nn.sigmoid(bproj)
    xa = aproj + dtb_ref[...]
    softplus = jnp.maximum(xa, 0.0) + jnp.log1p(jnp.exp(-jnp.abs(xa)))
    g = -jnp.exp(alog_ref[...]) * softplus
    row = lax.broadcasted_iota(I32, (tg, tg), 0)
    col = lax.broadcasted_iota(I32, (tg, tg), 1)
    same_chunk = (row // LA_CHUNK) == (col // LA_CHUNK)
    tri = jnp.where((col <= row) & same_chunk, 1.0, 0.0).astype(F32)
    gc_ref[...] = jnp.dot(tri, g, preferred_element_type=F32, precision=lax.Precision.HIGHEST)


def gates(p_c, a_log, dt_bias, *, tg=512):
    m = p_c.shape[0]
    tg = min(tg, m)
    assert m % tg == 0 and tg % LA_CHUNK == 0
    pad = lambda v: jnp.pad(v.astype(F32), (0, LANES - v.shape[0])).reshape(1, LANES)
    return pl.pallas_call(
        functools.partial(_gates_kernel, tg=tg),
        grid=(m // tg,),
        in_specs=[
            pl.BlockSpec((tg, 2 * LANES), lambda i: (i, 0)),
            pl.BlockSpec((1, LANES), lambda i: (0, 0)),
            pl.BlockSpec((1, LANES), lambda i: (0, 0)),
        ],
        out_specs=[pl.BlockSpec((tg, LANES), lambda i: (i, 0)),
                   pl.BlockSpec((tg, LANES), lambda i: (i, 0))],
        out_shape=[jax.ShapeDtypeStruct((m, LANES), F32), jax.ShapeDtypeStruct((m, LANES), F32)],
        compiler_params=_cparams(("parallel",)),
        name="gates",
    )(p_c, pad(a_log), pad(dt_bias))


def _split_bf16(a):
    hi = a.astype(BF16)
    lo = (a - hi.astype(F32)).astype(BF16)
    return hi, lo


def _gdn_kernel(q_ref, k_ref, v_ref, z_ref, gcc_ref, bc_ref, gcr_ref, nw_ref, o_ref,
                state_ref, m_ref, b_ref, qe_ref, oacc_ref, cd_ref, *, tb, hpg, rep, unroll):
    c_len = LA_CHUNK
    d = LA_HEAD_DIM
    n_chunks = tb // c_len
    hq0 = pl.program_id(1) * hpg

    @pl.when(pl.program_id(2) == 0)
    def _():
        state_ref[...] = jnp.zeros_like(state_ref)

    row = lax.broadcasted_iota(I32, (c_len, 2 * c_len), 0)
    lane = lax.broadcasted_iota(I32, (c_len, 2 * c_len), 1)
    col = jnp.where(lane < c_len, lane, lane - c_len)
    causal = row >= col
    strict = row > col
    eye_hi = jnp.where((lane >= c_len) & (row == col), 1.0, 0.0).astype(F32)
    left = lane < c_len
    dot = functools.partial(jnp.dot, preferred_element_type=F32)
    tdot = lambda a, b_: lax.dot_general(a, b_, (((1,), (1,)), ((), ())), preferred_element_type=F32)

    def prepare(g, carry):
        chains = []
        for i in range(unroll):
            c = g * unroll + i
            rows = pl.ds(pl.multiple_of(c * c_len, c_len), c_len)
            gtile = gcc_ref[0, rows, :]
            btile = bc_ref[0, rows, :]
            for a in range(hpg):
                q = q_ref[0, rows, a * d:(a + 1) * d]
                k = k_ref[0, rows, a * d:(a + 1) * d]
                kk_w = tdot(k, jnp.concatenate([k, k], axis=0))
                qk_t = tdot(q, k)
                for j in range(rep):
                    chains.append(dict(c=c, rows=rows, ch=a * rep + j, hv=(hq0 + a) * rep + j,
                                       gtile=gtile, btile=btile, q=q, k=k, kk_w=kk_w, qk_t=qk_t))
        for s in chains:
            s['gcol'] = jnp.sum(jnp.where(lane == s['hv'], s['gtile'], 0.0), axis=1, keepdims=True)
            s['bcol'] = jnp.sum(jnp.where(lane == s['hv'], s['btile'], 0.0), axis=1, keepdims=True)
            grow = gcr_ref[0, s['ch'], pl.ds(s['c'], 1), :]
            s['decay'] = jnp.exp(jnp.where(causal, s['gcol'] - grow, -jnp.inf))
            pt = jnp.where(strict & left, -(s['kk_w'] * s['bcol'] * s['decay']), 0.0)
            s['pt'] = pt + eye_hi
        for _ in range(6):
            for s in chains:
                s['ptb'] = s['pt'].astype(BF16)
            for s in chains:
                s['x'] = dot(s['ptb'][:, :c_len], s['ptb'])
            for s in chains:
                s['pt'] = jnp.where(left, s['x'], s['pt'] + s['x'])
        for s in chains:
            tmat = pltpu.roll(s['pt'], c_len, 1)[:, :c_len].astype(BF16)
            s['egc'] = jnp.exp(s['gcol'])
            s['kf'] = s['k'].astype(F32)
            vf = v_ref[0, s['rows'], s['ch'] * d:(s['ch'] + 1) * d].astype(F32)
            rhs = jnp.concatenate([vf * s['bcol'], s['kf'] * (s['bcol'] * s['egc'])], axis=1).astype(BF16)
            s['sol'] = dot(tmat, rhs)
        for s in chains:
            sol = s['sol'].astype(BF16)
            glast = s['gcol'][c_len - 1:c_len, :]
            kd = (s['kf'] * jnp.exp(glast - s['gcol'])).astype(BF16)
            aqk = (s['qk_t'] * s['decay'][:, :c_len]).astype(BF16)
            s['big'] = lax.dot_general(kd, sol, (((0,), (0,)), ((), ())), preferred_element_type=F32)
            s['small'] = dot(aqk, sol)
            s['cd'] = jnp.exp(glast)
        for s in chains:
            ch, c, rows = s['ch'], s['c'], s['rows']
            b_ref[ch, c] = s['big'][:, :d]
            m_ref[ch, c] = s['big'][:, d:].astype(BF16)
            oacc_ref[ch, rows, :] = s['small'][:, :d]
            qe_ref[ch, rows, :] = (s['q'].astype(F32) * s['egc'] - s['small'][:, d:]).astype(BF16)
            cd_ref[ch, pl.ds(c, 1), :] = jnp.broadcast_to(s['cd'], (1, LANES))
        return carry

    lax.fori_loop(0, n_chunks // unroll, prepare, 0)

    def recur(c, carry):
        rows = pl.ds(pl.multiple_of(c * c_len, c_len), c_len)
        for ch in range(hpg * rep):
            st = state_ref[ch]
            st_b = st.astype(BF16)
            oacc_ref[ch, rows, :] = oacc_ref[ch, rows, :] + dot(qe_ref[ch, rows, :], st_b)
            state_ref[ch] = st * cd_ref[ch, pl.ds(c, 1), :] - dot(m_ref[ch, c], st_b) + b_ref[ch, c]
        return carry

    lax.fori_loop(0, n_chunks, recur, 0)

    nw = nw_ref[...]
    for ch in range(hpg * rep):
        o = oacc_ref[ch]
        on = o * lax.rsqrt(jnp.mean(o * o, axis=-1, keepdims=True) + HEAD_NORM_EPS) * nw
        zf = z_ref[0, :, ch * d:(ch + 1) * d].astype(F32)
        o_ref[0, :, ch * d:(ch + 1) * d] = (on * (zf * jax.nn.sigmoid(zf))).astype(BF16)


def gdn(qkv, p_a, gc, beta, gc_rows, norm_w, *, tb=512, hpg=4, unroll=2):
    b, s, _ = qkv.shape
    tb = min(tb, s)
    assert s % tb == 0 and tb % (LA_CHUNK * unroll) == 0 and LA_QK_HEADS % hpg == 0
    rep = LA_V_HEADS // LA_QK_HEADS
    d = LA_HEAD_DIM
    qw = hpg * d
    vw = hpg * rep * d
    nch = hpg * rep
    n_chunks = tb // LA_CHUNK
    return pl.pallas_call(
        functools.partial(_gdn_kernel, tb=tb, hpg=hpg, rep=rep, unroll=unroll),
        grid=(b, LA_QK_HEADS // hpg, s // tb),
        in_specs=[
            pl.BlockSpec((1, tb, qw), lambda bi, h, si: (bi, si, h)),
            pl.BlockSpec((1, tb, qw), lambda bi, h, si: (bi, si, LA_QK_DIM // qw + h)),
            pl.BlockSpec((1, tb, vw), lambda bi, h, si: (bi, si, 2 * LA_QK_DIM // vw + h)),
            pl.BlockSpec((1, tb, vw), lambda bi, h, si: (bi, si, LA_CONV_DIM // vw + h)),
            pl.BlockSpec((1, tb, LANES), lambda bi, h, si: (bi, si, 0)),
            pl.BlockSpec((1, tb, LANES), lambda bi, h, si: (bi, si, 0)),
            pl.BlockSpec((1, nch, n_chunks, 2 * LA_CHUNK), lambda bi, h, si: (bi, h, si, 0)),
            pl.BlockSpec((1, d), lambda bi, h, si: (0, 0)),
        ],
        out_specs=pl.BlockSpec((1, tb, vw), lambda bi, h, si: (bi, si, h)),
        out_shape=jax.ShapeDtypeStruct((b, s, LA_V_DIM), BF16),
        scratch_shapes=[
            pltpu.VMEM((nch, d, d), F32),
            pltpu.VMEM((nch, n_chunks, d, d), BF16),
            pltpu.VMEM((nch, n_chunks, d, d), F32),
            pltpu.VMEM((nch, tb, d), BF16),
            pltpu.VMEM((nch, tb, d), F32),
            pltpu.VMEM((nch, n_chunks, LANES), F32),
        ],
        compiler_params=_cparams(("parallel", "parallel", "arbitrary")),
        name="gdn",
    )(qkv, qkv, qkv, p_a, gc, beta, gc_rows, norm_w.reshape(1, d).astype(F32))


def _group_rms(x, gather_ref, spread_ref, width):
    dot = functools.partial(jnp.dot, preferred_element_type=F32)
    sq = _split_bf16(x * x)
    gather = gather_ref[0:width, :]
    ss = dot(sq[0], gather) + dot(sq[1], gather)
    r = _split_bf16(lax.rsqrt(ss * (1.0 / SWA_HEAD_DIM) + HEAD_NORM_EPS))
    spread = spread_ref[:, 0:width]
    return dot(r[0], spread) + dot(r[1], spread)


def _swa_kernel(sink_ref, q_ref, kp_ref, kc_ref, vp_ref, vc_ref, qw_ref, kw_ref, gather_ref, spread_ref,
                 o_ref):
    n = pl.program_id(1)
    blk = WINDOW
    dh = SWA_HEAD_DIM
    slab = 2 * dh
    pairs = SWA_GROUP // 2
    qi = lax.broadcasted_iota(I32, (blk, 2 * blk), 0)
    kj = lax.broadcasted_iota(I32, (blk, 2 * blk), 1)
    dist = qi + blk - kj
    valid = (dist >= 0) & (dist < WINDOW) & ((n > 0) | (kj >= blk))
    dist_or_inf = jnp.where(valid, dist.astype(F32), jnp.inf)
    lane = lax.broadcasted_iota(I32, (2 * blk, slab), 1)
    low = lane < dh

    xq = q_ref[0].astype(F32)
    qn = (xq * _group_rms(xq, gather_ref, spread_ref, SWA_Q_DIM) * qw_ref[...]).astype(BF16)
    xk = jnp.concatenate([kp_ref[0], kc_ref[0]], axis=0).astype(F32)
    kn = xk * _group_rms(xk, gather_ref, spread_ref, SWA_KV_DIM) * kw_ref[...]
    xv = jnp.concatenate([vp_ref[0], vc_ref[0]], axis=0).astype(F32)

    def block_diag(x, g):
        s = x[:, (g // 2) * slab:(g // 2 + 1) * slab]
        other = pltpu.roll(s, dh, 1)
        top, bottom = (s, other) if g % 2 == 0 else (other, s)
        return jnp.concatenate([jnp.where(low, top, 0.0), jnp.where(low, 0.0, bottom)], axis=0).astype(BF16)

    for g in range(SWA_KV_HEADS):
        k_bd = block_diag(kn, g)
        v_bd = block_diag(xv, g)
        q4 = jnp.concatenate([qn[:, (g * pairs + p) * slab:(g * pairs + p + 1) * slab] for p in range(pairs)],
                             axis=0)
        sc = lax.dot_general(q4, k_bd, (((1,), (1,)), ((), ())), preferred_element_type=F32)
        rows = []
        for p in range(pairs):
            halves = []
            for a in range(2):
                h = g * SWA_GROUP + 2 * p + a
                slope = 2.0 ** (-8.0 * (h + 1) / SWA_Q_HEADS)
                sink = sink_ref[h]
                s_h = sc[p * blk:(p + 1) * blk, a * 2 * blk:(a + 1) * 2 * blk] - slope * dist_or_inf
                m = jnp.maximum(jnp.max(s_h, axis=-1, keepdims=True), sink)
                e = jnp.exp(s_h - m)
                denom = jnp.sum(e, axis=-1, keepdims=True) + jnp.exp(sink - m)
                halves.append((e / denom).astype(BF16))
            rows.append(jnp.concatenate(halves, axis=1))
        og = jnp.dot(jnp.concatenate(rows, axis=0), v_bd, preferred_element_type=F32)
        for p in range(pairs):
            c0 = (g * pairs + p) * slab
            o_ref[0, :, c0:c0 + slab] = og[p * blk:(p + 1) * blk, :].astype(BF16)


def swa(p_b, sinks, q_norm_w, k_norm_w):
    b, s, _ = p_b.shape
    blk = WINDOW
    dh = SWA_HEAD_DIM
    assert s % blk == 0
    kblk = SWA_Q_DIM // SWA_KV_DIM
    prev = lambda bi, n: jnp.maximum(n - 1, 0)
    qw = jnp.tile(q_norm_w.astype(F32) * dh ** -0.5, SWA_Q_HEADS).reshape(1, SWA_Q_DIM)
    kw = jnp.tile(k_norm_w.astype(F32), SWA_KV_HEADS).reshape(1, SWA_KV_DIM)
    head_of_col = jnp.arange(SWA_Q_DIM, dtype=I32) // dh
    gather = (head_of_col[:, None] == jnp.arange(LANES, dtype=I32)[None, :]).astype(BF16)
    fixed = lambda bi, n: (0, 0)
    return pl.pallas_call(
        _swa_kernel,
        grid=(b, s // blk),
        in_specs=[
            pl.BlockSpec(memory_space=pltpu.SMEM),
            pl.BlockSpec((1, blk, SWA_Q_DIM), lambda bi, n: (bi, n, 0)),
            pl.BlockSpec((1, blk, SWA_KV_DIM), lambda bi, n: (bi, prev(bi, n), kblk)),
            pl.BlockSpec((1, blk, SWA_KV_DIM), lambda bi, n: (bi, n, kblk)),
            pl.BlockSpec((1, blk, SWA_KV_DIM), lambda bi, n: (bi, prev(bi, n), kblk + 1)),
            pl.BlockSpec((1, blk, SWA_KV_DIM), lambda bi, n: (bi, n, kblk + 1)),
            pl.BlockSpec((1, SWA_Q_DIM), fixed),
            pl.BlockSpec((1, SWA_KV_DIM), fixed),
            pl.BlockSpec((SWA_Q_DIM, LANES), fixed),
            pl.BlockSpec((LANES, SWA_Q_DIM), fixed),
        ],
        out_specs=pl.BlockSpec((1, blk, SWA_Q_DIM), lambda bi, n: (bi, n, 0)),
        out_shape=jax.ShapeDtypeStruct((b, s, SWA_Q_DIM), BF16),
        compiler_params=_cparams(("parallel", "parallel")),
        name="swa",
    )(sinks.astype(F32), p_b, p_b, p_b, p_b, p_b, qw, kw, gather, gather.T)


def _merge_kernel(a_ref, wla_ref, bm_ref, wsw_ref, gla_ref, gsw_ref, bla_ref, bsw_ref, o_ref):
    oa = jnp.dot(a_ref[...], wla_ref[...], preferred_element_type=F32)
    ob = jnp.dot(bm_ref[...], wsw_ref[...], preferred_element_type=F32)
    gl = jax.nn.sigmoid(gla_ref[...].astype(F32) + bla_ref[...])
    gs = jax.nn.sigmoid(gsw_ref[...].astype(F32) + bsw_ref[...])
    o_ref[...] = (gl * oa + gs * ob).astype(BF16)


def merge(o_a, w_la, o_b, w_sw, p_b, b_gate, *, tm=1024, tn=512):
    m, ka = o_a.shape
    kb = o_b.shape[1]
    dm = w_la.shape[1]
    tm = min(tm, m)
    assert m % tm == 0 and dm % tn == 0
    gate0 = (SWA_Q_DIM + 2 * SWA_KV_DIM) // tn
    assert gate0 * tn == SWA_Q_DIM + 2 * SWA_KV_DIM
    nblk = dm // tn
    bg = b_gate.astype(F32).reshape(1, 2 * dm)
    return pl.pallas_call(
        _merge_kernel,
        grid=(m // tm, nblk),
        in_specs=[
            pl.BlockSpec((tm, ka), lambda i, j: (i, 0)),
            pl.BlockSpec((ka, tn), lambda i, j: (0, j)),
            pl.BlockSpec((tm, kb), lambda i, j: (i, 0)),
            pl.BlockSpec((kb, tn), lambda i, j: (0, j)),
            pl.BlockSpec((tm, tn), lambda i, j: (i, gate0 + j)),
            pl.BlockSpec((tm, tn), lambda i, j: (i, gate0 + nblk + j)),
            pl.BlockSpec((1, tn), lambda i, j: (0, j)),
            pl.BlockSpec((1, tn), lambda i, j: (0, nblk + j)),
        ],
        out_specs=pl.BlockSpec((tm, tn), lambda i, j: (i, j)),
        out_shape=jax.ShapeDtypeStruct((m, dm), BF16),
        compiler_params=_cparams(("parallel", "arbitrary")),
        name="merge",
    )(o_a, w_la, o_b, w_sw, p_b, p_b, bg, bg)


def _pack_rows(x):
    bits = lax.bitcast_convert_type(x.astype(BF16).astype(F32), U32)
    half = x.shape[1] // 2
    return (bits[:, :half] >> U32(16)) | (bits[:, half:] & U32(HIGH_HALF))


def _unpack_rows(w):
    lo = lax.bitcast_convert_type(w << U32(16), F32)
    hi = lax.bitcast_convert_type(w & U32(HIGH_HALF), F32)
    return lo, hi


def _oproj_router_kernel(x_ref, m_ref, wo_ref, g_ref, wr_ref, br_ref,
                         x1_ref, h2_ref, idx_ref, wgt_ref, rank_ref, cnt_ref, carry_ref, *, tm):
    @pl.when(pl.program_id(0) == 0)
    def _():
        carry_ref[...] = jnp.zeros_like(carry_ref)

    x1 = x_ref[...] + jnp.dot(m_ref[...], wo_ref[...], preferred_element_type=F32)
    x1_ref[...] = x1
    var = jnp.mean(x1 * x1, axis=-1, keepdims=True)
    h2 = x1 * lax.rsqrt(var + NORM_EPS) * g_ref[...]
    h2_ref[...] = _pack_rows(h2)
    lane = lax.broadcasted_iota(I32, (tm, LANES), 1)
    logits = jnp.dot(h2.astype(BF16), wr_ref[...], preferred_element_type=F32) + br_ref[...]
    logits = jnp.where(lane < N_EXPERTS, logits, -jnp.inf)
    vals, idxs = [], []
    for _ in range(TOP_K):
        mx = jnp.max(logits, axis=-1, keepdims=True)
        ix = jnp.min(jnp.where(logits == mx, lane, LANES), axis=-1, keepdims=True)
        vals.append(mx)
        idxs.append(ix)
        logits = jnp.where(lane == ix, -jnp.inf, logits)
    exps = [jnp.exp(v - vals[0]) for v in vals]
    tot = exps[0]
    for e in exps[1:]:
        tot = tot + e
    multihot = jnp.zeros((tm, LANES), F32)
    for ix in idxs:
        multihot = multihot + jnp.where(lane == ix, 1.0, 0.0)
    row = lax.broadcasted_iota(I32, (tm, tm), 0)
    col = lax.broadcasted_iota(I32, (tm, tm), 1)
    lower = jnp.where(col < row, 1.0, 0.0).astype(BF16)
    before = jnp.dot(lower, multihot.astype(BF16), preferred_element_type=F32) + carry_ref[...]
    idx_o = jnp.zeros((tm, LANES), I32)
    wgt_o = jnp.zeros((tm, LANES), F32)
    rank_o = jnp.zeros((tm, LANES), I32)
    for kk in range(TOP_K):
        rk = jnp.sum(jnp.where(lane == idxs[kk], before, 0.0), axis=-1, keepdims=True)
        idx_o = jnp.where(lane == kk, idxs[kk], idx_o)
        wgt_o = jnp.where(lane == kk, exps[kk] / tot, wgt_o)
        rank_o = jnp.where(lane == kk, rk.astype(I32), rank_o)
    idx_ref[...] = idx_o
    wgt_ref[...] = wgt_o
    rank_ref[...] = rank_o
    carry = carry_ref[...] + jnp.sum(multihot, axis=0, keepdims=True)
    carry_ref[...] = carry
    cnt_ref[...] = carry.astype(I32)


def oproj_router(x, m_act, w_o, gain, w_router, b_router, *, tm=256):
    m, dm = x.shape
    tm = min(tm, m)
    assert m % tm == 0
    wr = jnp.pad(w_router.astype(BF16), ((0, 0), (0, LANES - N_EXPERTS)))
    br = jnp.pad(b_router.astype(F32), (0, LANES - N_EXPERTS)).reshape(1, LANES)
    tok = lambda i: (i, 0)
    fixed = lambda i: (0, 0)
    return pl.pallas_call(
        functools.partial(_oproj_router_kernel, tm=tm),
        grid=(m // tm,),
        in_specs=[
            pl.BlockSpec((tm, dm), tok),
            pl.BlockSpec((tm, dm), tok),
            pl.BlockSpec((dm, dm), fixed),
            pl.BlockSpec((1, dm), fixed),
            pl.BlockSpec((dm, LANES), fixed),
            pl.BlockSpec((1, LANES), fixed),
        ],
        out_specs=[
            pl.BlockSpec((tm, dm), tok),
            pl.BlockSpec((tm, dm // 2), tok),
            pl.BlockSpec((tm, LANES), tok),
            pl.BlockSpec((tm, LANES), tok),
            pl.BlockSpec((tm, LANES), tok),
            pl.BlockSpec((1, LANES), fixed),
        ],
        out_shape=[
            jax.ShapeDtypeStruct((m, dm), F32),
            jax.ShapeDtypeStruct((m, dm // 2), U32),
            jax.ShapeDtypeStruct((m, LANES), I32),
            jax.ShapeDtypeStruct((m, LANES), F32),
            jax.ShapeDtypeStruct((m, LANES), I32),
            jax.ShapeDtypeStruct((1, LANES), I32),
        ],
        scratch_shapes=[pltpu.VMEM((1, LANES), F32)],
        compiler_params=_cparams(("arbitrary",)),
        name="oproj_router",
    )(x, m_act, w_o, gain.reshape(1, dm).astype(F32), wr, br)


def _row_copy(src, dst, sem, src_row, dst_row):
    return pltpu.make_async_copy(src.at[pl.ds(src_row, 1), :], dst.at[pl.ds(dst_row, 1), :], sem)


def _dispatch_kernel(pos_ref, h_ref, zero_hbm, xs_hbm, sem, *, td):
    del zero_hbm

    def issue(i, carry):
        for kk in range(TOP_K):
            _row_copy(h_ref, xs_hbm, sem, i, pos_ref[i * TOP_K + kk]).start()
        return carry

    lax.fori_loop(0, td, issue, 0, unroll=4)

    def drain(i, carry):
        _row_copy(h_ref, xs_hbm, sem, 0, 0).wait()
        return carry

    lax.fori_loop(0, td * TOP_K, drain, 0, unroll=8)


def dispatch(h_rows, pos_flat, n_rows, *, td=256):
    m, width = h_rows.shape
    td = min(td, m)
    assert m % td == 0
    zeros = jnp.zeros((n_rows, width), h_rows.dtype)
    return pl.pallas_call(
        functools.partial(_dispatch_kernel, td=td),
        grid=(m // td,),
        in_specs=[
            pl.BlockSpec((td * TOP_K,), lambda i: (i,), memory_space=pltpu.SMEM),
            pl.BlockSpec((td, width), lambda i: (i, 0)),
            pl.BlockSpec(memory_space=pl.ANY),
        ],
        out_specs=pl.BlockSpec(memory_space=pl.ANY),
        out_shape=jax.ShapeDtypeStruct(zeros.shape, zeros.dtype),
        scratch_shapes=[pltpu.SemaphoreType.DMA(())],
        input_output_aliases={2: 0},
        compiler_params=_cparams(("arbitrary",)),
        name="dispatch",
    )(pos_flat, h_rows, zeros)


def _deinterleave_kernel(w_ref, g_ref, l_ref):
    half = MXU_TILE // 2
    r = lax.broadcasted_iota(I32, (MXU_TILE, MXU_TILE), 0)
    c = lax.broadcasted_iota(I32, (MXU_TILE, MXU_TILE), 1)
    perm = jnp.where(r == jnp.where(c < half, 2 * c, 2 * (c - half) + 1), 1.0, 0.0).astype(BF16)
    for blk in range(w_ref.shape[2] // MXU_TILE):
        piece = w_ref[0, :, blk * MXU_TILE:(blk + 1) * MXU_TILE].astype(BF16)
        out = jnp.dot(piece, perm, preferred_element_type=F32)
        g_ref[0, :, blk * half:(blk + 1) * half] = out[:, :half].astype(BF16)
        l_ref[0, :, blk * half:(blk + 1) * half] = out[:, half:].astype(BF16)


def deinterleave(w, *, tk=512, tn=1024):
    e, k, n2 = w.shape
    n = n2 // 2
    assert k % tk == 0 and n % tn == 0 and (2 * tn) % MXU_TILE == 0
    out = jax.ShapeDtypeStruct((e, k, n), BF16)
    return pl.pallas_call(
        _deinterleave_kernel,
        grid=(e, k // tk, n // tn),
        in_specs=[pl.BlockSpec((1, tk, 2 * tn), lambda ei, ki, ni: (ei, ki, ni))],
        out_specs=[pl.BlockSpec((1, tk, tn), lambda ei, ki, ni: (ei, ki, ni)),
                   pl.BlockSpec((1, tk, tn), lambda ei, ki, ni: (ei, ki, ni))],
        out_shape=[out, out],
        compiler_params=_cparams(("parallel", "parallel", "parallel")),
        name="deinterleave",
    )(w)


def _experts_kernel(be_ref, nv_ref, x_ref, w1g_ref, w1l_ref, b1g_ref, b1l_ref, w2_ref, b2_ref,
                    o_ref, acc_ref, xb_ref, act_ref, *, nj):
    i = pl.program_id(0)
    j = pl.program_id(1)
    live = i < nv_ref[0]
    half = x_ref.shape[1]

    @pl.when(live & (j == 0))
    def _():
        lo, hi = _unpack_rows(x_ref[...])
        xb_ref[:, :half] = lo.astype(BF16)
        xb_ref[:, half:] = hi.astype(BF16)

    @pl.when(live)
    def _():
        x = xb_ref[...]
        hg = jnp.dot(x, w1g_ref[0], preferred_element_type=F32) + b1g_ref[0]
        hl = jnp.dot(x, w1l_ref[0], preferred_element_type=F32) + b1l_ref[0]
        glu = jnp.minimum(hg, SWIGLU_LIMIT)
        lin = jnp.clip(hl, -SWIGLU_LIMIT, SWIGLU_LIMIT)
        act_ref[...] = (glu * jax.nn.sigmoid(SWIGLU_ALPHA * glu) * (lin + 1.0)).astype(BF16)

    down = lambda: jnp.dot(act_ref[...], w2_ref[0], preferred_element_type=F32)
    first = live & (j == 0)
    last = live & (j == nj - 1)
    if nj == 1:
        @pl.when(first)
        def _():
            o_ref[...] = _pack_rows(down() + b2_ref[0])
    else:
        @pl.when(first)
        def _():
            acc_ref[...] = down() + b2_ref[0]

        @pl.when(live & (j > 0) & (j < nj - 1))
        def _():
            acc_ref[...] = acc_ref[...] + down()

        @pl.when(last)
        def _():
            o_ref[...] = _pack_rows(acc_ref[...] + down())

    @pl.when(jnp.logical_not(live) & (j == nj - 1))
    def _():
        o_ref[...] = jnp.zeros_like(o_ref)


def experts(xs, blk_expert, n_live, w1g, w1l, b1g, b1l, w2, b2, *, tj=1024):
    n_rows, half = xs.shape
    dm = 2 * half
    n_blk = n_rows // MOE_ROWS
    de = w2.shape[1]
    assert de % tj == 0
    nj = de // tj

    def wsel(i, j, be, nv):
        return be[i], jnp.where(i < nv[0], j, nj - 1)

    grid_spec = pltpu.PrefetchScalarGridSpec(
        num_scalar_prefetch=2,
        grid=(n_blk, nj),
        in_specs=[
            pl.BlockSpec((MOE_ROWS, half), lambda i, j, be, nv: (jnp.clip(i, 0, jnp.maximum(nv[0] - 1, 0)), 0)),
            pl.BlockSpec((1, dm, tj), lambda i, j, be, nv: (wsel(i, j, be, nv)[0], 0, wsel(i, j, be, nv)[1])),
            pl.BlockSpec((1, dm, tj), lambda i, j, be, nv: (wsel(i, j, be, nv)[0], 0, wsel(i, j, be, nv)[1])),
            pl.BlockSpec((1, 1, tj), lambda i, j, be, nv: (wsel(i, j, be, nv)[0], 0, wsel(i, j, be, nv)[1])),
            pl.BlockSpec((1, 1, tj), lambda i, j, be, nv: (wsel(i, j, be, nv)[0], 0, wsel(i, j, be, nv)[1])),
            pl.BlockSpec((1, tj, dm), lambda i, j, be, nv: (wsel(i, j, be, nv)[0], wsel(i, j, be, nv)[1], 0)),
            pl.BlockSpec((1, 1, dm), lambda i, j, be, nv: (be[i], 0, 0)),
        ],
        out_specs=pl.BlockSpec((MOE_ROWS, half), lambda i, j, be, nv: (i, 0)),
        scratch_shapes=[pltpu.VMEM((MOE_ROWS, dm), F32), pltpu.VMEM((MOE_ROWS, dm), BF16),
                        pltpu.VMEM((MOE_ROWS, tj), BF16)],
    )
    return pl.pallas_call(
        functools.partial(_experts_kernel, nj=nj),
        grid_spec=grid_spec,
        out_shape=jax.ShapeDtypeStruct((n_rows, half), U32),
        compiler_params=_cparams(("arbitrary", "arbitrary")),
        name="experts",
    )(blk_expert, n_live, xs, w1g, w1l, b1g, b1l, w2, b2)


def _combine_kernel(pos_ref, pos_next_ref, x1_ref, wgt_ref, y_hbm, o_ref, ybuf_ref, sem, *, tt):
    i = pl.program_id(0)
    slot = lax.rem(i, 2)
    half = ybuf_ref.shape[-1]

    def gather(p_ref, sl):
        def issue(t, carry):
            for kk in range(TOP_K):
                _row_copy(y_hbm, ybuf_ref.at[sl, kk], sem.at[sl], p_ref[t * TOP_K + kk], t).start()
            return carry

        lax.fori_loop(0, tt, issue, 0, unroll=4)

    @pl.when(i == 0)
    def _():
        gather(pos_ref, 0)

    @pl.when(i + 1 < pl.num_programs(0))
    def _():
        gather(pos_next_ref, 1 - slot)

    def drain(r, carry):
        _row_copy(y_hbm, ybuf_ref.at[slot, 0], sem.at[slot], 0, 0).wait()
        return carry

    lax.fori_loop(0, tt * TOP_K, drain, 0, unroll=8)

    w = wgt_ref[...]
    acc_lo = x1_ref[:, :half]
    acc_hi = x1_ref[:, half:]
    for kk in range(TOP_K):
        lo, hi = _unpack_rows(ybuf_ref[slot, kk])
        wk = w[:, kk:kk + 1]
        acc_lo = acc_lo + wk * lo
        acc_hi = acc_hi + wk * hi
    o_ref[:, :half] = acc_lo
    o_ref[:, half:] = acc_hi


def combine(x1, y_rows, pos_flat, wgt, *, tt=256):
    m, dm = x1.shape
    half = y_rows.shape[1]
    tt = min(tt, m)
    assert m % tt == 0 and dm == 2 * half
    n_tiles = m // tt
    return pl.pallas_call(
        functools.partial(_combine_kernel, tt=tt),
        grid=(n_tiles,),
        in_specs=[
            pl.BlockSpec((tt * TOP_K,), lambda i: (i,), memory_space=pltpu.SMEM),
            pl.BlockSpec((tt * TOP_K,), lambda i: (jnp.minimum(i + 1, n_tiles - 1),), memory_space=pltpu.SMEM),
            pl.BlockSpec((tt, dm), lambda i: (i, 0)),
            pl.BlockSpec((tt, LANES), lambda i: (i, 0)),
            pl.BlockSpec(memory_space=pl.ANY),
        ],
        out_specs=pl.BlockSpec((tt, dm), lambda i: (i, 0)),
        out_shape=jax.ShapeDtypeStruct((m, dm), F32),
        scratch_shapes=[pltpu.VMEM((2, TOP_K, tt, half), y_rows.dtype), pltpu.SemaphoreType.DMA((2,))],
        compiler_params=_cparams(("arbitrary",)),
        name="combine",
    )(pos_flat, pos_flat, x1, wgt, y_rows)


def _layer(x, ln_mix_w, w_in, b_gate, conv_w, a_log, dt_bias, la_norm_w, w_out_la, q_norm_w,
           k_norm_w, sinks, w_out_swa, w_o, ln_ffn_w, w_router, b_router, w1, b1, w2, b2):
    bsz, seq, dm = x.shape
    n_tok = bsz * seq
    xf = x.reshape(n_tok, dm)

    c_z_end = LA_CONV_DIM + LA_V_DIM
    c_ba_end = c_z_end + 2 * LA_V_HEADS
    w_a = w_in[:, :c_z_end].astype(BF16)
    w_b = w_in[:, c_ba_end:].astype(BF16)
    lane_pad = ((0, 0), (0, LANES - LA_V_HEADS))
    w_c = jnp.concatenate([jnp.pad(w_in[:, c_z_end:c_z_end + LA_V_HEADS], lane_pad),
                           jnp.pad(w_in[:, c_z_end + LA_V_HEADS:c_ba_end], lane_pad)], axis=1).astype(BF16)

    p_a = rms_matmul(xf, ln_mix_w, w_a, out_dtype=BF16, tm=1024, tn=1024)
    p_b = rms_matmul(xf, ln_mix_w, w_b, out_dtype=BF16, tm=1024, tn=512)
    p_c = rms_matmul(xf, ln_mix_w, w_c, out_dtype=F32, tm=1024, tn=2 * LANES)

    p_a3 = p_a.reshape(bsz, seq, c_z_end)
    qkv = conv_prep(p_a3, conv_w)
    beta, gc = gates(p_c, a_log, dt_bias)
    gc_rows = gc[:, :LA_V_HEADS].reshape(bsz, seq // LA_CHUNK, LA_CHUNK, LA_V_HEADS).transpose(0, 3, 1, 2)
    gc_rows = jnp.concatenate([gc_rows, gc_rows], axis=-1)
    o_a = gdn(qkv, p_a3, gc.reshape(bsz, seq, LANES), beta.reshape(bsz, seq, LANES), gc_rows, la_norm_w)

    p_b3 = p_b.reshape(bsz, seq, p_b.shape[1])
    o_b = swa(p_b3, sinks, q_norm_w, k_norm_w)

    m_act = merge(o_a.reshape(n_tok, LA_V_DIM), w_out_la.astype(BF16),
                  o_b.reshape(n_tok, SWA_Q_DIM), w_out_swa.astype(BF16), p_b, b_gate)
    x1, h2, idx, wgt, rank, counts = oproj_router(xf, m_act, w_o.astype(BF16), ln_ffn_w, w_router, b_router)

    counts = counts[0, :N_EXPERTS]
    padded = (counts + MOE_ROWS - 1) // MOE_ROWS * MOE_ROWS
    pend = jnp.cumsum(padded)
    pstart = pend - padded
    n_blk = n_tok * TOP_K // MOE_ROWS + N_EXPERTS
    n_rows = n_blk * MOE_ROWS
    n_live = (pend[-1:] // MOE_ROWS).astype(I32)
    blk_start = jnp.minimum(jnp.arange(n_blk, dtype=I32), jnp.maximum(n_live[0] - 1, 0)) * MOE_ROWS
    blk_expert = jnp.minimum(jnp.sum(pend[None, :] <= blk_start[:, None], axis=1), N_EXPERTS - 1).astype(I32)
    pos_flat = (pstart[idx[:, :TOP_K]] + rank[:, :TOP_K]).reshape(n_tok * TOP_K).astype(I32)

    xs = dispatch(h2, pos_flat, n_rows)
    de = w2.shape[1]
    w1g, w1l = deinterleave(w1)
    y = experts(xs, blk_expert, n_live, w1g, w1l,
                b1[:, 0::2].reshape(N_EXPERTS, 1, de).astype(F32), b1[:, 1::2].reshape(N_EXPERTS, 1, de).astype(F32),
                w2.astype(BF16), b2.reshape(N_EXPERTS, 1, dm).astype(F32))
    out = combine(x1, y, pos_flat, wgt)
    return out.reshape(bsz, seq, dm)


def kernel(x, ln_mix_w, w_in, b_gate, conv_w, a_log, dt_bias, la_norm_w, w_out_la, q_norm_w, k_norm_w,
           sinks, w_out_swa, w_o, ln_ffn_w, w_router, b_router, w1, b1, w2, b2):
    params = (ln_mix_w, w_in, b_gate, conv_w, a_log, dt_bias, la_norm_w, w_out_la, q_norm_w, k_norm_w,
              sinks, w_out_swa, w_o, ln_ffn_w, w_router, b_router, w1, b1, w2, b2)
    for layer in range(ln_mix_w.shape[0]):
        x = _layer(x, *(p[layer] for p in params))
    return x
```

```python
import functools

import jax
import jax.numpy as jnp
from jax import lax
from jax.experimental import pallas as pl
from jax.experimental.pallas import tpu as pltpu

F32 = jnp.float32
BF16 = jnp.bfloat16
I32 = jnp.int32

LA_QK_HEADS = 16
LA_V_HEADS = 32
LA_HEAD_DIM = 128
LA_CONV = 4
LA_CHUNK = 64
LA_QK_DIM = LA_QK_HEADS * LA_HEAD_DIM
LA_V_DIM = LA_V_HEADS * LA_HEAD_DIM
LA_CONV_DIM = 2 * LA_QK_DIM + LA_V_DIM
SWA_Q_HEADS = 32
SWA_KV_HEADS = 4
SWA_GROUP = SWA_Q_HEADS // SWA_KV_HEADS
SWA_HEAD_DIM = 64
SWA_Q_DIM = SWA_Q_HEADS * SWA_HEAD_DIM
SWA_KV_DIM = SWA_KV_HEADS * SWA_HEAD_DIM
WINDOW = 128
N_EXPERTS = 32
TOP_K = 4
SWIGLU_LIMIT = 7.0
SWIGLU_ALPHA = 1.702
NORM_EPS = 1e-5
HEAD_NORM_EPS = 1e-6
L2_EPS = 1e-6

U32 = jnp.uint32

LANES = 128
ROW_TILE = 16
MXU_TILE = 256
CONV_ROWS = 128
HIGH_HALF = 0xFFFF0000
MOE_ROWS = 512
VMEM_LIMIT = 56 * 1024 * 1024


def _cparams(sem, vmem=VMEM_LIMIT):
    return pltpu.CompilerParams(dimension_semantics=sem, vmem_limit_bytes=vmem)


def _rms_matmul_kernel(x_ref, g_ref, w_ref, o_ref, h_ref, *, eps, rows):
    @pl.when(pl.program_id(1) == 0)
    def _():
        for r in range(0, x_ref.shape[0], rows):
            x = x_ref[r:r + rows, :]
            var = jnp.mean(x * x, axis=-1, keepdims=True)
            h_ref[r:r + rows, :] = (x * lax.rsqrt(var + eps) * g_ref[...]).astype(BF16)

    o_ref[...] = jnp.dot(h_ref[...], w_ref[...], preferred_element_type=F32).astype(o_ref.dtype)


def rms_matmul(x, gain, w, *, out_dtype, tm, tn):
    m, k = x.shape
    n = w.shape[1]
    tm = min(tm, m)
    tn = min(tn, n)
    assert m % tm == 0 and n % tn == 0
    return pl.pallas_call(
        functools.partial(_rms_matmul_kernel, eps=NORM_EPS, rows=min(256, tm)),
        grid=(m // tm, n // tn),
        in_specs=[
            pl.BlockSpec((tm, k), lambda i, j: (i, 0)),
            pl.BlockSpec((1, k), lambda i, j: (0, 0)),
            pl.BlockSpec((k, tn), lambda i, j: (0, j)),
        ],
        out_specs=pl.BlockSpec((tm, tn), lambda i, j: (i, j)),
        out_shape=jax.ShapeDtypeStruct((m, n), out_dtype),
        scratch_shapes=[pltpu.VMEM((tm, k), BF16)],
        compiler_params=_cparams(("parallel", "arbitrary")),
        name="rms_matmul",
    )(x, gain.reshape(1, k), w)


def _conv_kernel(x_ref, halo_ref, w_ref, shift_ref, o_ref, xe_ref, *, tc, cw, n_q_tiles, n_qk_tiles):
    s = pl.program_id(1)
    c = pl.program_id(2)
    rb = CONV_ROWS
    taps = LA_CONV - 1
    xe_ref[0:ROW_TILE, :] = jnp.where(s > 0, halo_ref[0], jnp.zeros_like(halo_ref[0]))
    xe_ref[ROW_TILE:ROW_TILE + tc, :] = x_ref[0]
    xe_ref[ROW_TILE + tc:, :] = jnp.zeros((xe_ref.shape[0] - ROW_TILE - tc, cw), BF16)
    q_scale = jnp.where(c < n_q_tiles, LA_HEAD_DIM ** -0.5, 1.0)
    for blk in range(tc // rb):
        rows = slice(blk * rb, (blk + 1) * rb)
        shifted = jnp.dot(shift_ref[...], xe_ref[blk * rb:blk * rb + 2 * rb, :],
                          preferred_element_type=F32)
        acc = w_ref[taps:taps + 1, :] * x_ref[0, rows, :].astype(F32)
        for j in range(taps):
            acc = acc + w_ref[j:j + 1, :] * shifted[j * rb:(j + 1) * rb, :]
        y = acc * jax.nn.sigmoid(acc)
        for hh in range(cw // LA_HEAD_DIM):
            yh = y[:, hh * LA_HEAD_DIM:(hh + 1) * LA_HEAD_DIM]
            ss = jnp.sum(yh * yh, axis=-1, keepdims=True)
            f = jnp.where(c < n_qk_tiles, lax.rsqrt(ss + L2_EPS), 1.0) * q_scale
            o_ref[0, rows, hh * LA_HEAD_DIM:(hh + 1) * LA_HEAD_DIM] = (yh * f).astype(BF16)


def conv_prep(p_a, conv_w, *, tc=512, cw=512):
    b, s, _ = p_a.shape
    tc = min(tc, s)
    rb = CONV_ROWS
    taps = LA_CONV - 1
    assert s % tc == 0 and tc % rb == 0 and LA_QK_DIM % cw == 0 and taps <= ROW_TILE
    halo_blocks = tc // ROW_TILE
    t_idx = jnp.arange(taps * rb, dtype=I32) % rb
    j_idx = jnp.arange(taps * rb, dtype=I32) // rb
    shift = (jnp.arange(2 * rb, dtype=I32)[None, :] == (t_idx + ROW_TILE - taps + j_idx)[:, None]).astype(BF16)
    return pl.pallas_call(
        functools.partial(_conv_kernel, tc=tc, cw=cw, n_q_tiles=LA_QK_DIM // cw,
                          n_qk_tiles=2 * LA_QK_DIM // cw),
        grid=(b, s // tc, LA_CONV_DIM // cw),
        in_specs=[
            pl.BlockSpec((1, tc, cw), lambda bi, si, ci: (bi, si, ci)),
            pl.BlockSpec((1, ROW_TILE, cw),
                         lambda bi, si, ci: (bi, jnp.maximum(si * halo_blocks - 1, 0), ci)),
            pl.BlockSpec((LA_CONV, cw), lambda bi, si, ci: (0, ci)),
            pl.BlockSpec((taps * rb, 2 * rb), lambda bi, si, ci: (0, 0)),
        ],
        out_specs=pl.BlockSpec((1, tc, cw), lambda bi, si, ci: (bi, si, ci)),
        out_shape=jax.ShapeDtypeStruct((b, s, LA_CONV_DIM), BF16),
        scratch_shapes=[pltpu.VMEM((tc + rb, cw), BF16)],
        compiler_params=_cparams(("parallel", "parallel", "parallel")),
        name="conv_prep",
    )(p_a, p_a, conv_w, shift)


def _gates_kernel(p_ref, alog_ref, dtb_ref, beta_ref, gc_ref, *, tg):
    bproj = p_ref[:, 0:LANES]
    aproj = p_ref[:, LANES:2 * LANES]
    beta_ref[...] = jax.nn.sigmoid(bproj)
    xa = aproj + dtb_ref[...]
    softplus = jnp.maximum(xa, 0.0) + jnp.log1p(jnp.exp(-jnp.abs(xa)))
    g = -jnp.exp(alog_ref[...]) * softplus
    row = lax.broadcasted_iota(I32, (tg, tg), 0)
    col = lax.broadcasted_iota(I32, (tg, tg), 1)
    same_chunk = (row // LA_CHUNK) == (col // LA_CHUNK)
    tri = jnp.where((col <= row) & same_chunk, 1.0, 0.0).astype(F32)
    gc_ref[...] = jnp.dot(tri, g, preferred_element_type=F32, precision=lax.Precision.HIGHEST)


def gates(p_c, a_log, dt_bias, *, tg=512):
    m = p_c.shape[0]
    tg = min(tg, m)
    assert m % tg == 0 and tg % LA_CHUNK == 0
    pad = lambda v: jnp.pad(v.astype(F32), (0, LANES - v.shape[0])).reshape(1, LANES)
    return pl.pallas_call(
        functools.partial(_gates_kernel, tg=tg),
        grid=(m // tg,),
        in_specs=[
            pl.BlockSpec((tg, 2 * LANES), lambda i: (i, 0)),
            pl.BlockSpec((1, LANES), lambda i: (0, 0)),
            pl.BlockSpec((1, LANES), lambda i: (0, 0)),
        ],
        out_specs=[pl.BlockSpec((tg, LANES), lambda i: (i, 0)),
                   pl.BlockSpec((tg, LANES), lambda i: (i, 0))],
        out_shape=[jax.ShapeDtypeStruct((m, LANES), F32), jax.ShapeDtypeStruct((m, LANES), F32)],
        compiler_params=_cparams(("parallel",)),
        name="gates",
    )(p_c, pad(a_log), pad(dt_bias))


def _split_bf16(a):
    hi = a.astype(BF16)
    lo = (a - hi.astype(F32)).astype(BF16)
    return hi, lo


def _gdn_kernel(q_ref, k_ref, v_ref, z_ref, gcc_ref, bc_ref, gcr_ref, nw_ref, o_ref,
                state_ref, m_ref, b_ref, qe_ref, oacc_ref, cd_ref, *, tb, hpg, rep, unroll):
    c_len = LA_CHUNK
    d = LA_HEAD_DIM
    n_chunks = tb // c_len
    hq0 = pl.program_id(1) * hpg

    @pl.when(pl.program_id(2) == 0)
    def _():
        state_ref[...] = jnp.zeros_like(state_ref)

    row = lax.broadcasted_iota(I32, (c_len, 2 * c_len), 0)
    lane = lax.broadcasted_iota(I32, (c_len, 2 * c_len), 1)
    col = jnp.where(lane < c_len, lane, lane - c_len)
    causal = row >= col
    strict = row > col
    eye_hi = jnp.where((lane >= c_len) & (row == col), 1.0, 0.0).astype(F32)
    left = lane < c_len
    dot = functools.partial(jnp.dot, preferred_element_type=F32)
    tdot = lambda a, b_: lax.dot_general(a, b_, (((1,), (1,)), ((), ())), preferred_element_type=F32)

    def prepare(g, carry):
        chains = []
        for i in range(unroll):
            c = g * unroll + i
            rows = pl.ds(pl.multiple_of(c * c_len, c_len), c_len)
            gtile = gcc_ref[0, rows, :]
            btile = bc_ref[0, rows, :]
            for a in range(hpg):
                q = q_ref[0, rows, a * d:(a + 1) * d]
                k = k_ref[0, rows, a * d:(a + 1) * d]
                kk_w = tdot(k, jnp.concatenate([k, k], axis=0))
                qk_t = tdot(q, k)
                for j in range(rep):
                    chains.append(dict(c=c, rows=rows, ch=a * rep + j, hv=(hq0 + a) * rep + j,
                                       gtile=gtile, btile=btile, q=q, k=k, kk_w=kk_w, qk_t=qk_t))
        for s in chains:
            s['gcol'] = jnp.sum(jnp.where(lane == s['hv'], s['gtile'], 0.0), axis=1, keepdims=True)
            s['bcol'] = jnp.sum(jnp.where(lane == s['hv'], s['btile'], 0.0), axis=1, keepdims=True)
            grow = gcr_ref[0, s['ch'], pl.ds(s['c'], 1), :]
            s['decay'] = jnp.exp(jnp.where(causal, s['gcol'] - grow, -jnp.inf))
            pt = jnp.where(strict & left, -(s['kk_w'] * s['bcol'] * s['decay']), 0.0)
            s['pt'] = pt + eye_hi
        for _ in range(6):
            for s in chains:
                s['ptb'] = s['pt'].astype(BF16)
            for s in chains:
                s['x'] = dot(s['ptb'][:, :c_len], s['ptb'])
            for s in chains:
                s['pt'] = jnp.where(left, s['x'], s['pt'] + s['x'])
        for s in chains:
            tmat = pltpu.roll(s['pt'], c_len, 1)[:, :c_len].astype(BF16)
            s['egc'] = jnp.exp(s['gcol'])
            s['kf'] = s['k'].astype(F32)
            vf = v_ref[0, s['rows'], s['ch'] * d:(s['ch'] + 1) * d].astype(F32)
            rhs = jnp.concatenate([vf * s['bcol'], s['kf'] * (s['bcol'] * s['egc'])], axis=1).astype(BF16)
            s['sol'] = dot(tmat, rhs)
        for s in chains:
            sol = s['sol'].astype(BF16)
            glast = s['gcol'][c_len - 1:c_len, :]
            kd = (s['kf'] * jnp.exp(glast - s['gcol'])).astype(BF16)
            aqk = (s['qk_t'] * s['decay'][:, :c_len]).astype(BF16)
            s['big'] = lax.dot_general(kd, sol, (((0,), (0,)), ((), ())), preferred_element_type=F32)
            s['small'] = dot(aqk, sol)
            s['cd'] = jnp.exp(glast)
        for s in chains:
            ch, c, rows = s['ch'], s['c'], s['rows']
            b_ref[ch, c] = s['big'][:, :d]
            m_ref[ch, c] = s['big'][:, d:].astype(BF16)
            oacc_ref[ch, rows, :] = s['small'][:, :d]
            qe_ref[ch, rows, :] = (s['q'].astype(F32) * s['egc'] - s['small'][:, d:]).astype(BF16)
            cd_ref[ch, pl.ds(c, 1), :] = jnp.broadcast_to(s['cd'], (1, LANES))
        return carry

    lax.fori_loop(0, n_chunks // unroll, prepare, 0)

    def recur(c, carry):
        rows = pl.ds(pl.multiple_of(c * c_len, c_len), c_len)
        for ch in range(hpg * rep):
            st = state_ref[ch]
            st_b = st.astype(BF16)
            oacc_ref[ch, rows, :] = oacc_ref[ch, rows, :] + dot(qe_ref[ch, rows, :], st_b)
            state_ref[ch] = st * cd_ref[ch, pl.ds(c, 1), :] - dot(m_ref[ch, c], st_b) + b_ref[ch, c]
        return carry

    lax.fori_loop(0, n_chunks, recur, 0)

    nw = nw_ref[...]
    for ch in range(hpg * rep):
        o = oacc_ref[ch]
        on = o * lax.rsqrt(jnp.mean(o * o, axis=-1, keepdims=True) + HEAD_NORM_EPS) * nw
        zf = z_ref[0, :, ch * d:(ch + 1) * d].astype(F32)
        o_ref[0, :, ch * d:(ch + 1) * d] = (on * (zf * jax.nn.sigmoid(zf))).astype(BF16)


def gdn(qkv, p_a, gc, beta, gc_rows, norm_w, *, tb=512, hpg=4, unroll=2):
    b, s, _ = qkv.shape
    tb = min(tb, s)
    assert s % tb == 0 and tb % (LA_CHUNK * unroll) == 0 and LA_QK_HEADS % hpg == 0
    rep = LA_V_HEADS // LA_QK_HEADS
    d = LA_HEAD_DIM
    qw = hpg * d
    vw = hpg * rep * d
    nch = hpg * rep
    n_chunks = tb // LA_CHUNK
    return pl.pallas_call(
        functools.partial(_gdn_kernel, tb=tb, hpg=hpg, rep=rep, unroll=unroll),
        grid=(b, LA_QK_HEADS // hpg, s // tb),
        in_specs=[
            pl.BlockSpec((1, tb, qw), lambda bi, h, si: (bi, si, h)),
            pl.BlockSpec((1, tb, qw), lambda bi, h, si: (bi, si, LA_QK_DIM // qw + h)),
            pl.BlockSpec((1, tb, vw), lambda bi, h, si: (bi, si, 2 * LA_QK_DIM // vw + h)),
            pl.BlockSpec((1, tb, vw), lambda bi, h, si: (bi, si, LA_CONV_DIM // vw + h)),
            pl.BlockSpec((1, tb, LANES), lambda bi, h, si: (bi, si, 0)),
            pl.BlockSpec((1, tb, LANES), lambda bi, h, si: (bi, si, 0)),
            pl.BlockSpec((1, nch, n_chunks, 2 * LA_CHUNK), lambda bi, h, si: (bi, h, si, 0)),
            pl.BlockSpec((1, d), lambda bi, h, si: (0, 0)),
        ],
        out_specs=pl.BlockSpec((1, tb, vw), lambda bi, h, si: (bi, si, h)),
        out_shape=jax.ShapeDtypeStruct((b, s, LA_V_DIM), BF16),
        scratch_shapes=[
            pltpu.VMEM((nch, d, d), F32),
            pltpu.VMEM((nch, n_chunks, d, d), BF16),
            pltpu.VMEM((nch, n_chunks, d, d), F32),
            pltpu.VMEM((nch, tb, d), BF16),
            pltpu.VMEM((nch, tb, d), F32),
            pltpu.VMEM((nch, n_chunks, LANES), F32),
        ],
        compiler_params=_cparams(("parallel", "parallel", "arbitrary")),
        name="gdn",
    )(qkv, qkv, qkv, p_a, gc, beta, gc_rows, norm_w.reshape(1, d).astype(F32))


def _group_rms(x, gather_ref, spread_ref, width):
    dot = functools.partial(jnp.dot, preferred_element_type=F32)
    sq = _split_bf16(x * x)
    gather = gather_ref[0:width, :]
    ss = dot(sq[0], gather) + dot(sq[1], gather)
    r = _split_bf16(lax.rsqrt(ss * (1.0 / SWA_HEAD_DIM) + HEAD_NORM_EPS))
    spread = spread_ref[:, 0:width]
    return dot(r[0], spread) + dot(r[1], spread)


def _swa_kernel(sink_ref, q_ref, kp_ref, kc_ref, vp_ref, vc_ref, qw_ref, kw_ref, gather_ref, spread_ref,
                 o_ref):
    n = pl.program_id(1)
    blk = WINDOW
    dh = SWA_HEAD_DIM
    slab = 2 * dh
    pairs = SWA_GROUP // 2
    qi = lax.broadcasted_iota(I32, (blk, 2 * blk), 0)
    kj = lax.broadcasted_iota(I32, (blk, 2 * blk), 1)
    dist = qi + blk - kj
    valid = (dist >= 0) & (dist < WINDOW) & ((n > 0) | (kj >= blk))
    dist_or_inf = jnp.where(valid, dist.astype(F32), jnp.inf)
    lane = lax.broadcasted_iota(I32, (2 * blk, slab), 1)
    low = lane < dh

    xq = q_ref[0].astype(F32)
    qn = (xq * _group_rms(xq, gather_ref, spread_ref, SWA_Q_DIM) * qw_ref[...]).astype(BF16)
    xk = jnp.concatenate([kp_ref[0], kc_ref[0]], axis=0).astype(F32)
    kn = xk * _group_rms(xk, gather_ref, spread_ref, SWA_KV_DIM) * kw_ref[...]
    xv = jnp.concatenate([vp_ref[0], vc_ref[0]], axis=0).astype(F32)

    def block_diag(x, g):
        s = x[:, (g // 2) * slab:(g // 2 + 1) * slab]
        other = pltpu.roll(s, dh, 1)
        top, bottom = (s, other) if g % 2 == 0 else (other, s)
        return jnp.concatenate([jnp.where(low, top, 0.0), jnp.where(low, 0.0, bottom)], axis=0).astype(BF16)

    for g in range(SWA_KV_HEADS):
        k_bd = block_diag(kn, g)
        v_bd = block_diag(xv, g)
        q4 = jnp.concatenate([qn[:, (g * pairs + p) * slab:(g * pairs + p + 1) * slab] for p in range(pairs)],
                             axis=0)
        sc = lax.dot_general(q4, k_bd, (((1,), (1,)), ((), ())), preferred_element_type=F32)
        rows = []
        for p in range(pairs):
            halves = []
            for a in range(2):
                h = g * SWA_GROUP + 2 * p + a
                slope = 2.0 ** (-8.0 * (h + 1) / SWA_Q_HEADS)
                sink = sink_ref[h]
                s_h = sc[p * blk:(p + 1) * blk, a * 2 * blk:(a + 1) * 2 * blk] - slope * dist_or_inf
                m = jnp.maximum(jnp.max(s_h, axis=-1, keepdims=True), sink)
                e = jnp.exp(s_h - m)
                denom = jnp.sum(e, axis=-1, keepdims=True) + jnp.exp(sink - m)
                halves.append((e / denom).astype(BF16))
            rows.append(jnp.concatenate(halves, axis=1))
        og = jnp.dot(jnp.concatenate(rows, axis=0), v_bd, preferred_element_type=F32)
        for p in range(pairs):
            c0 = (g * pairs + p) * slab
            o_ref[0, :, c0:c0 + slab] = og[p * blk:(p + 1) * blk, :].astype(BF16)


def swa(p_b, sinks, q_norm_w, k_norm_w):
    b, s, _ = p_b.shape
    blk = WINDOW
    dh = SWA_HEAD_DIM
    assert s % blk == 0
    kblk = SWA_Q_DIM // SWA_KV_DIM
    prev = lambda bi, n: jnp.maximum(n - 1, 0)
    qw = jnp.tile(q_norm_w.astype(F32) * dh ** -0.5, SWA_Q_HEADS).reshape(1, SWA_Q_DIM)
    kw = jnp.tile(k_norm_w.astype(F32), SWA_KV_HEADS).reshape(1, SWA_KV_DIM)
    head_of_col = jnp.arange(SWA_Q_DIM, dtype=I32) // dh
    gather = (head_of_col[:, None] == jnp.arange(LANES, dtype=I32)[None, :]).astype(BF16)
    fixed = lambda bi, n: (0, 0)
    return pl.pallas_call(
        _swa_kernel,
        grid=(b, s // blk),
        in_specs=[
            pl.BlockSpec(memory_space=pltpu.SMEM),
            pl.BlockSpec((1, blk, SWA_Q_DIM), lambda bi, n: (bi, n, 0)),
            pl.BlockSpec((1, blk, SWA_KV_DIM), lambda bi, n: (bi, prev(bi, n), kblk)),
            pl.BlockSpec((1, blk, SWA_KV_DIM), lambda bi, n: (bi, n, kblk)),
            pl.BlockSpec((1, blk, SWA_KV_DIM), lambda bi, n: (bi, prev(bi, n), kblk + 1)),
            pl.BlockSpec((1, blk, SWA_KV_DIM), lambda bi, n: (bi, n, kblk + 1)),
            pl.BlockSpec((1, SWA_Q_DIM), fixed),
            pl.BlockSpec((1, SWA_KV_DIM), fixed),
            pl.BlockSpec((SWA_Q_DIM, LANES), fixed),
            pl.BlockSpec((LANES, SWA_Q_DIM), fixed),
        ],
        out_specs=pl.BlockSpec((1, blk, SWA_Q_DIM), lambda bi, n: (bi, n, 0)),
        out_shape=jax.ShapeDtypeStruct((b, s, SWA_Q_DIM), BF16),
        compiler_params=_cparams(("parallel", "parallel")),
        name="swa",
    )(sinks.astype(F32), p_b, p_b, p_b, p_b, p_b, qw, kw, gather, gather.T)


def _merge_kernel(a_ref, wla_ref, bm_ref, wsw_ref, gla_ref, gsw_ref, bla_ref, bsw_ref, o_ref):
    oa = jnp.dot(a_ref[...], wla_ref[...], preferred_element_type=F32)
    ob = jnp.dot(bm_ref[...], wsw_ref[...], preferred_element_type=F32)
    gl = jax.nn.sigmoid(gla_ref[...].astype(F32) + bla_ref[...])
    gs = jax.nn.sigmoid(gsw_ref[...].astype(F32) + bsw_ref[...])
    o_ref[...] = (gl * oa + gs * ob).astype(BF16)


def merge(o_a, w_la, o_b, w_sw, p_b, b_gate, *, tm=1024, tn=512):
    m, ka = o_a.shape
    kb = o_b.shape[1]
    dm = w_la.shape[1]
    tm = min(tm, m)
    assert m % tm == 0 and dm % tn == 0
    gate0 = (SWA_Q_DIM + 2 * SWA_KV_DIM) // tn
    assert gate0 * tn == SWA_Q_DIM + 2 * SWA_KV_DIM
    nblk = dm // tn
    bg = b_gate.astype(F32).reshape(1, 2 * dm)
    return pl.pallas_call(
        _merge_kernel,
        grid=(m // tm, nblk),
        in_specs=[
            pl.BlockSpec((tm, ka), lambda i, j: (i, 0)),
            pl.BlockSpec((ka, tn), lambda i, j: (0, j)),
            pl.BlockSpec((tm, kb), lambda i, j: (i, 0)),
            pl.BlockSpec((kb, tn), lambda i, j: (0, j)),
            pl.BlockSpec((tm, tn), lambda i, j: (i, gate0 + j)),
            pl.BlockSpec((tm, tn), lambda i, j: (i, gate0 + nblk + j)),
            pl.BlockSpec((1, tn), lambda i, j: (0, j)),
            pl.BlockSpec((1, tn), lambda i, j: (0, nblk + j)),
        ],
        out_specs=pl.BlockSpec((tm, tn), lambda i, j: (i, j)),
        out_shape=jax.ShapeDtypeStruct((m, dm), BF16),
        compiler_params=_cparams(("parallel", "arbitrary")),
        name="merge",
    )(o_a, w_la, o_b, w_sw, p_b, p_b, bg, bg)


def _pack_rows(x):
    bits = lax.bitcast_convert_type(x.astype(BF16).astype(F32), U32)
    half = x.shape[1] // 2
    return (bits[:, :half] >> U32(16)) | (bits[:, half:] & U32(HIGH_HALF))


def _unpack_rows(w):
    lo = lax.bitcast_convert_type(w << U32(16), F32)
    hi = lax.bitcast_convert_type(w & U32(HIGH_HALF), F32)
    return lo, hi


def _oproj_router_kernel(x_ref, m_ref, wo_ref, g_ref, wr_ref, br_ref,
                         x1_ref, h2_ref, idx_ref, wgt_ref, rank_ref, cnt_ref, carry_ref, *, tm):
    @pl.when(pl.program_id(0) == 0)
    def _():
        carry_ref[...] = jnp.zeros_like(carry_ref)

    x1 = x_ref[...] + jnp.dot(m_ref[...], wo_ref[...], preferred_element_type=F32)
    x1_ref[...] = x1
    var = jnp.mean(x1 * x1, axis=-1, keepdims=True)
    h2 = x1 * lax.rsqrt(var + NORM_EPS) * g_ref[...]
    h2_ref[...] = _pack_rows(h2)
    lane = lax.broadcasted_iota(I32, (tm, LANES), 1)
    logits = jnp.dot(h2.astype(BF16), wr_ref[...], preferred_element_type=F32) + br_ref[...]
    logits = jnp.where(lane < N_EXPERTS, logits, -jnp.inf)
    vals, idxs = [], []
    for _ in range(TOP_K):
        mx = jnp.max(logits, axis=-1, keepdims=True)
        ix = jnp.min(jnp.where(logits == mx, lane, LANES), axis=-1, keepdims=True)
        vals.append(mx)
        idxs.append(ix)
        logits = jnp.where(lane == ix, -jnp.inf, logits)
    exps = [jnp.exp(v - vals[0]) for v in vals]
    tot = exps[0]
    for e in exps[1:]:
        tot = tot + e
    multihot = jnp.zeros((tm, LANES), F32)
    for ix in idxs:
        multihot = multihot + jnp.where(lane == ix, 1.0, 0.0)
    row = lax.broadcasted_iota(I32, (tm, tm), 0)
    col = lax.broadcasted_iota(I32, (tm, tm), 1)
    lower = jnp.where(col < row, 1.0, 0.0).astype(BF16)
    before = jnp.dot(lower, multihot.astype(BF16), preferred_element_type=F32) + carry_ref[...]
    idx_o = jnp.zeros((tm, LANES), I32)
    wgt_o = jnp.zeros((tm, LANES), F32)
    rank_o = jnp.zeros((tm, LANES), I32)
    for kk in range(TOP_K):
        rk = jnp.sum(jnp.where(lane == idxs[kk], before, 0.0), axis=-1, keepdims=True)
        idx_o = jnp.where(lane == kk, idxs[kk], idx_o)
        wgt_o = jnp.where(lane == kk, exps[kk] / tot, wgt_o)
        rank_o = jnp.where(lane == kk, rk.astype(I32), rank_o)
    idx_ref[...] = idx_o
    wgt_ref[...] = wgt_o
    rank_ref[...] = rank_o
    carry = carry_ref[...] + jnp.sum(multihot, axis=0, keepdims=True)
    carry_ref[...] = carry
    cnt_ref[...] = carry.astype(I32)


def oproj_router(x, m_act, w_o, gain, w_router, b_router, *, tm=256):
    m, dm = x.shape
    tm = min(tm, m)
    assert m % tm == 0
    wr = jnp.pad(w_router.astype(BF16), ((0, 0), (0, LANES - N_EXPERTS)))
    br = jnp.pad(b_router.astype(F32), (0, LANES - N_EXPERTS)).reshape(1, LANES)
    tok = lambda i: (i, 0)
    fixed = lambda i: (0, 0)
    return pl.pallas_call(
        functools.partial(_oproj_router_kernel, tm=tm),
        grid=(m // tm,),
        in_specs=[
            pl.BlockSpec((tm, dm), tok),
            pl.BlockSpec((tm, dm), tok),
            pl.BlockSpec((dm, dm), fixed),
            pl.BlockSpec((1, dm), fixed),
            pl.BlockSpec((dm, LANES), fixed),
            pl.BlockSpec((1, LANES), fixed),
        ],
        out_specs=[
            pl.BlockSpec((tm, dm), tok),
            pl.BlockSpec((tm, dm // 2), tok),
            pl.BlockSpec((tm, LANES), tok),
            pl.BlockSpec((tm, LANES), tok),
            pl.BlockSpec((tm, LANES), tok),
            pl.BlockSpec((1, LANES), fixed),
        ],
        out_shape=[
            jax.ShapeDtypeStruct((m, dm), F32),
            jax.ShapeDtypeStruct((m, dm // 2), U32),
            jax.ShapeDtypeStruct((m, LANES), I32),
            jax.ShapeDtypeStruct((m, LANES), F32),
            jax.ShapeDtypeStruct((m, LANES), I32),
            jax.ShapeDtypeStruct((1, LANES), I32),
        ],
        scratch_shapes=[pltpu.VMEM((1, LANES), F32)],
        compiler_params=_cparams(("arbitrary",)),
        name="oproj_router",
    )(x, m_act, w_o, gain.reshape(1, dm).astype(F32), wr, br)


def _row_copy(src, dst, sem, src_row, dst_row):
    return pltpu.make_async_copy(src.at[pl.ds(src_row, 1), :], dst.at[pl.ds(dst_row, 1), :], sem)


def _dispatch_kernel(pos_ref, h_ref, zero_hbm, xs_hbm, sem, *, td):
    del zero_hbm

    def issue(i, carry):
        for kk in range(TOP_K):
            _row_copy(h_ref, xs_hbm, sem, i, pos_ref[i * TOP_K + kk]).start()
        return carry

    lax.fori_loop(0, td, issue, 0, unroll=4)

    def drain(i, carry):
        _row_copy(h_ref, xs_hbm, sem, 0, 0).wait()
        return carry

    lax.fori_loop(0, td * TOP_K, drain, 0, unroll=8)


def dispatch(h_rows, pos_flat, n_rows, *, td=512):
    m, width = h_rows.shape
    td = min(td, m)
    assert m % td == 0
    zeros = jnp.zeros((n_rows, width), h_rows.dtype)
    return pl.pallas_call(
        functools.partial(_dispatch_kernel, td=td),
        grid=(m // td,),
        in_specs=[
            pl.BlockSpec((td * TOP_K,), lambda i: (i,), memory_space=pltpu.SMEM),
            pl.BlockSpec((td, width), lambda i: (i, 0)),
            pl.BlockSpec(memory_space=pl.ANY),
        ],
        out_specs=pl.BlockSpec(memory_space=pl.ANY),
        out_shape=jax.ShapeDtypeStruct(zeros.shape, zeros.dtype),
        scratch_shapes=[pltpu.SemaphoreType.DMA(())],
        input_output_aliases={2: 0},
        compiler_params=_cparams(("arbitrary",)),
        name="dispatch",
    )(pos_flat, h_rows, zeros)


def _deinterleave_kernel(w_ref, g_ref, l_ref):
    half = MXU_TILE // 2
    r = lax.broadcasted_iota(I32, (MXU_TILE, MXU_TILE), 0)
    c = lax.broadcasted_iota(I32, (MXU_TILE, MXU_TILE), 1)
    perm = jnp.where(r == jnp.where(c < half, 2 * c, 2 * (c - half) + 1), 1.0, 0.0).astype(BF16)
    for blk in range(w_ref.shape[2] // MXU_TILE):
        piece = w_ref[0, :, blk * MXU_TILE:(blk + 1) * MXU_TILE].astype(BF16)
        out = jnp.dot(piece, perm, preferred_element_type=F32)
        g_ref[0, :, blk * half:(blk + 1) * half] = out[:, :half].astype(BF16)
        l_ref[0, :, blk * half:(blk + 1) * half] = out[:, half:].astype(BF16)


def deinterleave(w, *, tk=512, tn=1024):
    e, k, n2 = w.shape
    n = n2 // 2
    assert k % tk == 0 and n % tn == 0 and (2 * tn) % MXU_TILE == 0
    out = jax.ShapeDtypeStruct((e, k, n), BF16)
    return pl.pallas_call(
        _deinterleave_kernel,
        grid=(e, k // tk, n // tn),
        in_specs=[pl.BlockSpec((1, tk, 2 * tn), lambda ei, ki, ni: (ei, ki, ni))],
        out_specs=[pl.BlockSpec((1, tk, tn), lambda ei, ki, ni: (ei, ki, ni)),
                   pl.BlockSpec((1, tk, tn), lambda ei, ki, ni: (ei, ki, ni))],
        out_shape=[out, out],
        compiler_params=_cparams(("parallel", "parallel", "parallel")),
        name="deinterleave",
    )(w)


def _experts_kernel(be_ref, nv_ref, x_ref, w1g_ref, w1l_ref, b1g_ref, b1l_ref, w2_ref, b2_ref,
                    o_ref, acc_ref, xb_ref, act_ref, *, nj):
    i = pl.program_id(0)
    j = pl.program_id(1)
    live = i < nv_ref[0]
    half = x_ref.shape[1]

    @pl.when(live & (j == 0))
    def _():
        lo, hi = _unpack_rows(x_ref[...])
        xb_ref[:, :half] = lo.astype(BF16)
        xb_ref[:, half:] = hi.astype(BF16)

    @pl.when(live)
    def _():
        x = xb_ref[...]
        hg = jnp.dot(x, w1g_ref[0], preferred_element_type=F32) + b1g_ref[0]
        hl = jnp.dot(x, w1l_ref[0], preferred_element_type=F32) + b1l_ref[0]
        glu = jnp.minimum(hg, SWIGLU_LIMIT)
        lin = jnp.clip(hl, -SWIGLU_LIMIT, SWIGLU_LIMIT)
        act_ref[...] = (glu * jax.nn.sigmoid(SWIGLU_ALPHA * glu) * (lin + 1.0)).astype(BF16)

    down = lambda: jnp.dot(act_ref[...], w2_ref[0], preferred_element_type=F32)
    first = live & (j == 0)
    last = live & (j == nj - 1)
    if nj == 1:
        @pl.when(first)
        def _():
            o_ref[...] = _pack_rows(down() + b2_ref[0])
    else:
        @pl.when(first)
        def _():
            acc_ref[...] = down() + b2_ref[0]

        @pl.when(live & (j > 0) & (j < nj - 1))
        def _():
            acc_ref[...] = acc_ref[...] + down()

        @pl.when(last)
        def _():
            o_ref[...] = _pack_rows(acc_ref[...] + down())

    @pl.when(jnp.logical_not(live) & (j == nj - 1))
    def _():
        o_ref[...] = jnp.zeros_like(o_ref)


def experts(xs, blk_expert, n_live, w1g, w1l, b1g, b1l, w2, b2, *, tj=1024):
    n_rows, half = xs.shape
    dm = 2 * half
    n_blk = n_rows // MOE_ROWS
    de = w2.shape[1]
    assert de % tj == 0
    nj = de // tj

    def wsel(i, j, be, nv):
        return be[i], jnp.where(i < nv[0], j, nj - 1)

    grid_spec = pltpu.PrefetchScalarGridSpec(
        num_scalar_prefetch=2,
        grid=(n_blk, nj),
        in_specs=[
            pl.BlockSpec((MOE_ROWS, half), lambda i, j, be, nv: (jnp.clip(i, 0, jnp.maximum(nv[0] - 1, 0)), 0)),
            pl.BlockSpec((1, dm, tj), lambda i, j, be, nv: (wsel(i, j, be, nv)[0], 0, wsel(i, j, be, nv)[1])),
            pl.BlockSpec((1, dm, tj), lambda i, j, be, nv: (wsel(i, j, be, nv)[0], 0, wsel(i, j, be, nv)[1])),
            pl.BlockSpec((1, 1, tj), lambda i, j, be, nv: (wsel(i, j, be, nv)[0], 0, wsel(i, j, be, nv)[1])),
            pl.BlockSpec((1, 1, tj), lambda i, j, be, nv: (wsel(i, j, be, nv)[0], 0, wsel(i, j, be, nv)[1])),
            pl.BlockSpec((1, tj, dm), lambda i, j, be, nv: (wsel(i, j, be, nv)[0], wsel(i, j, be, nv)[1], 0)),
            pl.BlockSpec((1, 1, dm), lambda i, j, be, nv: (be[i], 0, 0)),
        ],
        out_specs=pl.BlockSpec((MOE_ROWS, half), lambda i, j, be, nv: (i, 0)),
        scratch_shapes=[pltpu.VMEM((MOE_ROWS, dm), F32), pltpu.VMEM((MOE_ROWS, dm), BF16),
                        pltpu.VMEM((MOE_ROWS, tj), BF16)],
    )
    return pl.pallas_call(
        functools.partial(_experts_kernel, nj=nj),
        grid_spec=grid_spec,
        out_shape=jax.ShapeDtypeStruct((n_rows, half), U32),
        compiler_params=_cparams(("arbitrary", "arbitrary")),
        name="experts",
    )(blk_expert, n_live, xs, w1g, w1l, b1g, b1l, w2, b2)


def _combine_kernel(pos_ref, pos_next_ref, x1_ref, wgt_ref, y_hbm, o_ref, ybuf_ref, sem, *, tt):
    i = pl.program_id(0)
    slot = lax.rem(i, 2)
    half = ybuf_ref.shape[-1]

    def gather(p_ref, sl):
        def issue(t, carry):
            for kk in range(TOP_K):
                _row_copy(y_hbm, ybuf_ref.at[sl, kk], sem.at[sl], p_ref[t * TOP_K + kk], t).start()
            return carry

        lax.fori_loop(0, tt, issue, 0, unroll=4)

    @pl.when(i == 0)
    def _():
        gather(pos_ref, 0)

    @pl.when(i + 1 < pl.num_programs(0))
    def _():
        gather(pos_next_ref, 1 - slot)

    def drain(r, carry):
        _row_copy(y_hbm, ybuf_ref.at[slot, 0], sem.at[slot], 0, 0).wait()
        return carry

    lax.fori_loop(0, tt * TOP_K, drain, 0, unroll=8)

    w = wgt_ref[...]
    acc_lo = x1_ref[:, :half]
    acc_hi = x1_ref[:, half:]
    for kk in range(TOP_K):
        lo, hi = _unpack_rows(ybuf_ref[slot, kk])
        wk = w[:, kk:kk + 1]
        acc_lo = acc_lo + wk * lo
        acc_hi = acc_hi + wk * hi
    o_ref[:, :half] = acc_lo
    o_ref[:, half:] = acc_hi


def combine(x1, y_rows, pos_flat, wgt, *, tt=256):
    m, dm = x1.shape
    half = y_rows.shape[1]
    tt = min(tt, m)
    assert m % tt == 0 and dm == 2 * half
    n_tiles = m // tt
    return pl.pallas_call(
        functools.partial(_combine_kernel, tt=tt),
        grid=(n_tiles,),
        in_specs=[
            pl.BlockSpec((tt * TOP_K,), lambda i: (i,), memory_space=pltpu.SMEM),
            pl.BlockSpec((tt * TOP_K,), lambda i: (jnp.minimum(i + 1, n_tiles - 1),), memory_space=pltpu.SMEM),
            pl.BlockSpec((tt, dm), lambda i: (i, 0)),
            pl.BlockSpec((tt, LANES), lambda i: (i, 0)),
            pl.BlockSpec(memory_space=pl.ANY),
        ],
        out_specs=pl.BlockSpec((tt, dm), lambda i: (i, 0)),
        out_shape=jax.ShapeDtypeStruct((m, dm), F32),
        scratch_shapes=[pltpu.VMEM((2, TOP_K, tt, half), y_rows.dtype), pltpu.SemaphoreType.DMA((2,))],
        compiler_params=_cparams(("arbitrary",)),
        name="combine",
    )(pos_flat, pos_flat, x1, wgt, y_rows)


def _layer(x, ln_mix_w, w_in, b_gate, conv_w, a_log, dt_bias, la_norm_w, w_out_la, q_norm_w,
           k_norm_w, sinks, w_out_swa, w_o, ln_ffn_w, w_router, b_router, w1, b1, w2, b2):
    bsz, seq, dm = x.shape
    n_tok = bsz * seq
    xf = x.reshape(n_tok, dm)

    c_z_end = LA_CONV_DIM + LA_V_DIM
    c_ba_end = c_z_end + 2 * LA_V_HEADS
    w_a = w_in[:, :c_z_end].astype(BF16)
    w_b = w_in[:, c_ba_end:].astype(BF16)
    lane_pad = ((0, 0), (0, LANES - LA_V_HEADS))
    w_c = jnp.concatenate([jnp.pad(w_in[:, c_z_end:c_z_end + LA_V_HEADS], lane_pad),
                           jnp.pad(w_in[:, c_z_end + LA_V_HEADS:c_ba_end], lane_pad)], axis=1).astype(BF16)

    p_a = rms_matmul(xf, ln_mix_w, w_a, out_dtype=BF16, tm=1024, tn=1024)
    p_b = rms_matmul(xf, ln_mix_w, w_b, out_dtype=BF16, tm=1024, tn=512)
    p_c = rms_matmul(xf, ln_mix_w, w_c, out_dtype=F32, tm=1024, tn=2 * LANES)

    p_a3 = p_a.reshape(bsz, seq, c_z_end)
    qkv = conv_prep(p_a3, conv_w)
    beta, gc = gates(p_c, a_log, dt_bias)
    gc_rows = gc[:, :LA_V_HEADS].reshape(bsz, seq // LA_CHUNK, LA_CHUNK, LA_V_HEADS).transpose(0, 3, 1, 2)
    gc_rows = jnp.concatenate([gc_rows, gc_rows], axis=-1)
    o_a = gdn(qkv, p_a3, gc.reshape(bsz, seq, LANES), beta.reshape(bsz, seq, LANES), gc_rows, la_norm_w)

    p_b3 = p_b.reshape(bsz, seq, p_b.shape[1])
    o_b = swa(p_b3, sinks, q_norm_w, k_norm_w)

    m_act = merge(o_a.reshape(n_tok, LA_V_DIM), w_out_la.astype(BF16),
                  o_b.reshape(n_tok, SWA_Q_DIM), w_out_swa.astype(BF16), p_b, b_gate)
    x1, h2, idx, wgt, rank, counts = oproj_router(xf, m_act, w_o.astype(BF16), ln_ffn_w, w_router, b_router)

    counts = counts[0, :N_EXPERTS]
    padded = (counts + MOE_ROWS - 1) // MOE_ROWS * MOE_ROWS
    pend = jnp.cumsum(padded)
    pstart = pend - padded
    n_blk = n_tok * TOP_K // MOE_ROWS + N_EXPERTS
    n_rows = n_blk * MOE_ROWS
    n_live = (pend[-1:] // MOE_ROWS).astype(I32)
    blk_start = jnp.minimum(jnp.arange(n_blk, dtype=I32), jnp.maximum(n_live[0] - 1, 0)) * MOE_ROWS
    blk_expert = jnp.minimum(jnp.sum(pend[None, :] <= blk_start[:, None], axis=1), N_EXPERTS - 1).astype(I32)
    pos_flat = (pstart[idx[:, :TOP_K]] + rank[:, :TOP_K]).reshape(n_tok * TOP_K).astype(I32)

    xs = dispatch(h2, pos_flat, n_rows)
    de = w2.shape[1]
    w1g, w1l = deinterleave(w1)
    y = experts(xs, blk_expert, n_live, w1g, w1l,
                b1[:, 0::2].reshape(N_EXPERTS, 1, de).astype(F32), b1[:, 1::2].reshape(N_EXPERTS, 1, de).astype(F32),
                w2.astype(BF16), b2.reshape(N_EXPERTS, 1, dm).astype(F32))
    out = combine(x1, y, pos_flat, wgt)
    return out.reshape(bsz, seq, dm)


def kernel(x, ln_mix_w, w_in, b_gate, conv_w, a_log, dt_bias, la_norm_w, w_out_la, q_norm_w, k_norm_w,
           sinks, w_out_swa, w_o, ln_ffn_w, w_router, b_router, w1, b1, w2, b2):
    params = (ln_mix_w, w_in, b_gate, conv_w, a_log, dt_bias, la_norm_w, w_out_la, q_norm_w, k_norm_w,
              sinks, w_out_swa, w_o, ln_ffn_w, w_router, b_router, w1, b1, w2, b2)
    for layer in range(ln_mix_w.shape[0]):
        x = _layer(x, *(p[layer] for p in params))
    return x
```

```python
import functools

import jax
import jax.numpy as jnp
from jax import lax
from jax.experimental import pallas as pl
from jax.experimental.pallas import tpu as pltpu

F32 = jnp.float32
BF16 = jnp.bfloat16
I32 = jnp.int32

LA_QK_HEADS = 16
LA_V_HEADS = 32
LA_HEAD_DIM = 128
LA_CONV = 4
LA_CHUNK = 64
LA_QK_DIM = LA_QK_HEADS * LA_HEAD_DIM
LA_V_DIM = LA_V_HEADS * LA_HEAD_DIM
LA_CONV_DIM = 2 * LA_QK_DIM + LA_V_DIM
SWA_Q_HEADS = 32
SWA_KV_HEADS = 4
SWA_GROUP = SWA_Q_HEADS // SWA_KV_HEADS
SWA_HEAD_DIM = 64
SWA_Q_DIM = SWA_Q_HEADS * SWA_HEAD_DIM
SWA_KV_DIM = SWA_KV_HEADS * SWA_HEAD_DIM
WINDOW = 128
N_EXPERTS = 32
TOP_K = 4
SWIGLU_LIMIT = 7.0
SWIGLU_ALPHA = 1.702
NORM_EPS = 1e-5
HEAD_NORM_EPS = 1e-6
L2_EPS = 1e-6

U32 = jnp.uint32

LANES = 128
ROW_TILE = 16
MXU_TILE = 256
CONV_ROWS = 128
HIGH_HALF = 0xFFFF0000
MOE_ROWS = 512
VMEM_LIMIT = 56 * 1024 * 1024


def _cparams(sem, vmem=VMEM_LIMIT):
    return pltpu.CompilerParams(dimension_semantics=sem, vmem_limit_bytes=vmem)


def _rms_matmul_kernel(x_ref, g_ref, w_ref, o_ref, h_ref, *, eps, rows):
    @pl.when(pl.program_id(1) == 0)
    def _():
        for r in range(0, x_ref.shape[0], rows):
            x = x_ref[r:r + rows, :]
            var = jnp.mean(x * x, axis=-1, keepdims=True)
            h_ref[r:r + rows, :] = (x * lax.rsqrt(var + eps) * g_ref[...]).astype(BF16)

    o_ref[...] = jnp.dot(h_ref[...], w_ref[...], preferred_element_type=F32).astype(o_ref.dtype)


def rms_matmul(x, gain, w, *, out_dtype, tm, tn, n=None):
    m, k = x.shape
    n = w.shape[1] if n is None else n
    tm = min(tm, m)
    tn = min(tn, n)
    assert m % tm == 0 and n % tn == 0
    return pl.pallas_call(
        functools.partial(_rms_matmul_kernel, eps=NORM_EPS, rows=min(256, tm)),
        grid=(m // tm, n // tn),
        in_specs=[
            pl.BlockSpec((tm, k), lambda i, j: (i, 0)),
            pl.BlockSpec((1, k), lambda i, j: (0, 0)),
            pl.BlockSpec((k, tn), lambda i, j: (0, j)),
        ],
        out_specs=pl.BlockSpec((tm, tn), lambda i, j: (i, j)),
        out_shape=jax.ShapeDtypeStruct((m, n), out_dtype),
        scratch_shapes=[pltpu.VMEM((tm, k), BF16)],
        compiler_params=_cparams(("parallel", "arbitrary")),
        name="rms_matmul",
    )(x, gain.reshape(1, k), w)


def _conv_kernel(x_ref, halo_ref, w_ref, shift_ref, o_ref, xe_ref, *, tc, cw, n_q_tiles, n_qk_tiles):
    s = pl.program_id(1)
    c = pl.program_id(2)
    rb = CONV_ROWS
    taps = LA_CONV - 1
    xe_ref[0:ROW_TILE, :] = jnp.where(s > 0, halo_ref[0], jnp.zeros_like(halo_ref[0]))
    xe_ref[ROW_TILE:ROW_TILE + tc, :] = x_ref[0]
    xe_ref[ROW_TILE + tc:, :] = jnp.zeros((xe_ref.shape[0] - ROW_TILE - tc, cw), BF16)
    q_scale = jnp.where(c < n_q_tiles, LA_HEAD_DIM ** -0.5, 1.0)
    for blk in range(tc // rb):
        rows = slice(blk * rb, (blk + 1) * rb)
        shifted = jnp.dot(shift_ref[...], xe_ref[blk * rb:blk * rb + 2 * rb, :],
                          preferred_element_type=F32)
        acc = w_ref[taps:taps + 1, :] * x_ref[0, rows, :].astype(F32)
        for j in range(taps):
            acc = acc + w_ref[j:j + 1, :] * shifted[j * rb:(j + 1) * rb, :]
        y = acc * jax.nn.sigmoid(acc)
        for hh in range(cw // LA_HEAD_DIM):
            yh = y[:, hh * LA_HEAD_DIM:(hh + 1) * LA_HEAD_DIM]
            ss = jnp.sum(yh * yh, axis=-1, keepdims=True)
            f = jnp.where(c < n_qk_tiles, lax.rsqrt(ss + L2_EPS), 1.0) * q_scale
            o_ref[0, rows, hh * LA_HEAD_DIM:(hh + 1) * LA_HEAD_DIM] = (yh * f).astype(BF16)


def conv_prep(p_a, conv_w, *, tc=512, cw=512):
    b, s, _ = p_a.shape
    tc = min(tc, s)
    rb = CONV_ROWS
    taps = LA_CONV - 1
    assert s % tc == 0 and tc % rb == 0 and LA_QK_DIM % cw == 0 and taps <= ROW_TILE
    halo_blocks = tc // ROW_TILE
    t_idx = jnp.arange(taps * rb, dtype=I32) % rb
    j_idx = jnp.arange(taps * rb, dtype=I32) // rb
    shift = (jnp.arange(2 * rb, dtype=I32)[None, :] == (t_idx + ROW_TILE - taps + j_idx)[:, None]).astype(BF16)
    return pl.pallas_call(
        functools.partial(_conv_kernel, tc=tc, cw=cw, n_q_tiles=LA_QK_DIM // cw,
                          n_qk_tiles=2 * LA_QK_DIM // cw),
        grid=(b, s // tc, LA_CONV_DIM // cw),
        in_specs=[
            pl.BlockSpec((1, tc, cw), lambda bi, si, ci: (bi, si, ci)),
            pl.BlockSpec((1, ROW_TILE, cw),
                         lambda bi, si, ci: (bi, jnp.maximum(si * halo_blocks - 1, 0), ci)),
            pl.BlockSpec((LA_CONV, cw), lambda bi, si, ci: (0, ci)),
            pl.BlockSpec((taps * rb, 2 * rb), lambda bi, si, ci: (0, 0)),
        ],
        out_specs=pl.BlockSpec((1, tc, cw), lambda bi, si, ci: (bi, si, ci)),
        out_shape=jax.ShapeDtypeStruct((b, s, LA_CONV_DIM), BF16),
        scratch_shapes=[pltpu.VMEM((tc + rb, cw), BF16)],
        compiler_params=_cparams(("parallel", "parallel", "parallel")),
        name="conv_prep",
    )(p_a, p_a, conv_w, shift)


def _gates_kernel(p_ref, alog_ref, dtb_ref, beta_ref, gc_ref, *, tg):
    bproj = p_ref[:, 0:LANES]
    aproj = p_ref[:, LANES:2 * LANES]
    beta_ref[...] = jax.nn.sigmoid(bproj)
    xa = aproj + dtb_ref[...]
    softplus = jnp.maximum(xa, 0.0) + jnp.log1p(jnp.exp(-jnp.abs(xa)))
    g = -jnp.exp(alog_ref[...]) * softplus
    row = lax.broadcasted_iota(I32, (tg, tg), 0)
    col = lax.broadcasted_iota(I32, (tg, tg), 1)
    same_chunk = (row // LA_CHUNK) == (col // LA_CHUNK)
    tri = jnp.where((col <= row) & same_chunk, 1.0, 0.0).astype(F32)
    gc_ref[...] = jnp.dot(tri, g, preferred_element_type=F32, precision=lax.Precision.HIGHEST)


def gates(p_c, a_log, dt_bias, *, tg=512):
    m = p_c.shape[0]
    tg = min(tg, m)
    assert m % tg == 0 and tg % LA_CHUNK == 0
    pad = lambda v: jnp.pad(v.astype(F32), (0, LANES - v.shape[0])).reshape(1, LANES)
    return pl.pallas_call(
        functools.partial(_gates_kernel, tg=tg),
        grid=(m // tg,),
        in_specs=[
            pl.BlockSpec((tg, 2 * LANES), lambda i: (i, 0)),
            pl.BlockSpec((1, LANES), lambda i: (0, 0)),
            pl.BlockSpec((1, LANES), lambda i: (0, 0)),
        ],
        out_specs=[pl.BlockSpec((tg, LANES), lambda i: (i, 0)),
                   pl.BlockSpec((tg, LANES), lambda i: (i, 0))],
        out_shape=[jax.ShapeDtypeStruct((m, LANES), F32), jax.ShapeDtypeStruct((m, LANES), F32)],
        compiler_params=_cparams(("parallel",)),
        name="gates",
    )(p_c, pad(a_log), pad(dt_bias))


def _split_bf16(a):
    hi = a.astype(BF16)
    lo = (a - hi.astype(F32)).astype(BF16)
    return hi, lo


def _gdn_kernel(q_ref, k_ref, v_ref, z_ref, gcc_ref, bc_ref, gcr_ref, nw_ref, o_ref,
                state_ref, m_ref, b_ref, qe_ref, oacc_ref, cd_ref, *, tb, hpg, rep, unroll):
    c_len = LA_CHUNK
    d = LA_HEAD_DIM
    n_chunks = tb // c_len
    hq0 = pl.program_id(1) * hpg

    @pl.when(pl.program_id(2) == 0)
    def _():
        state_ref[...] = jnp.zeros_like(state_ref)

    row = lax.broadcasted_iota(I32, (c_len, 2 * c_len), 0)
    lane = lax.broadcasted_iota(I32, (c_len, 2 * c_len), 1)
    col = jnp.where(lane < c_len, lane, lane - c_len)
    causal = row >= col
    strict = row > col
    eye_hi = jnp.where((lane >= c_len) & (row == col), 1.0, 0.0).astype(F32)
    left = lane < c_len
    dot = functools.partial(jnp.dot, preferred_element_type=F32)
    tdot = lambda a, b_: lax.dot_general(a, b_, (((1,), (1,)), ((), ())), preferred_element_type=F32)

    def prepare(g, carry):
        chains = []
        for i in range(unroll):
            c = g * unroll + i
            rows = pl.ds(pl.multiple_of(c * c_len, c_len), c_len)
            gtile = gcc_ref[0, rows, :]
            btile = bc_ref[0, rows, :]
            for a in range(hpg):
                q = q_ref[0, rows, a * d:(a + 1) * d]
                k = k_ref[0, rows, a * d:(a + 1) * d]
                kk_w = tdot(k, jnp.concatenate([k, k], axis=0))
                qk_t = tdot(q, k)
                for j in range(rep):
                    chains.append(dict(c=c, rows=rows, ch=a * rep + j, hv=(hq0 + a) * rep + j,
                                       gtile=gtile, btile=btile, q=q, k=k, kk_w=kk_w, qk_t=qk_t))
        for s in chains:
            s['gcol'] = jnp.sum(jnp.where(lane == s['hv'], s['gtile'], 0.0), axis=1, keepdims=True)
            s['bcol'] = jnp.sum(jnp.where(lane == s['hv'], s['btile'], 0.0), axis=1, keepdims=True)
            grow = gcr_ref[0, s['ch'], pl.ds(s['c'], 1), :]
            s['decay'] = jnp.exp(jnp.where(causal, s['gcol'] - grow, -jnp.inf))
            pt = jnp.where(strict & left, -(s['kk_w'] * s['bcol'] * s['decay']), 0.0)
            s['pt'] = pt + eye_hi
        for _ in range(6):
            for s in chains:
                s['ptb'] = s['pt'].astype(BF16)
            for s in chains:
                s['x'] = dot(s['ptb'][:, :c_len], s['ptb'])
            for s in chains:
                s['pt'] = jnp.where(left, s['x'], s['pt'] + s['x'])
        for s in chains:
            tmat = pltpu.roll(s['pt'], c_len, 1)[:, :c_len].astype(BF16)
            s['egc'] = jnp.exp(s['gcol'])
            s['kf'] = s['k'].astype(F32)
            vf = v_ref[0, s['rows'], s['ch'] * d:(s['ch'] + 1) * d].astype(F32)
            rhs = jnp.concatenate([vf * s['bcol'], s['kf'] * (s['bcol'] * s['egc'])], axis=1).astype(BF16)
            s['sol'] = dot(tmat, rhs)
        for s in chains:
            sol = s['sol'].astype(BF16)
            glast = s['gcol'][c_len - 1:c_len, :]
            kd = (s['kf'] * jnp.exp(glast - s['gcol'])).astype(BF16)
            aqk = (s['qk_t'] * s['decay'][:, :c_len]).astype(BF16)
            s['big'] = lax.dot_general(kd, sol, (((0,), (0,)), ((), ())), preferred_element_type=F32)
            s['small'] = dot(aqk, sol)
            s['cd'] = jnp.exp(glast)
        for s in chains:
            ch, c, rows = s['ch'], s['c'], s['rows']
            b_ref[ch, c] = s['big'][:, :d]
            m_ref[ch, c] = s['big'][:, d:].astype(BF16)
            oacc_ref[ch, rows, :] = s['small'][:, :d]
            qe_ref[ch, rows, :] = (s['q'].astype(F32) * s['egc'] - s['small'][:, d:]).astype(BF16)
            cd_ref[ch, pl.ds(c, 1), :] = jnp.broadcast_to(s['cd'], (1, LANES))
        return carry

    lax.fori_loop(0, n_chunks // unroll, prepare, 0)

    def recur(c, carry):
        rows = pl.ds(pl.multiple_of(c * c_len, c_len), c_len)
        for ch in range(hpg * rep):
            st = state_ref[ch]
            st_b = st.astype(BF16)
            oacc_ref[ch, rows, :] = oacc_ref[ch, rows, :] + dot(qe_ref[ch, rows, :], st_b)
            state_ref[ch] = st * cd_ref[ch, pl.ds(c, 1), :] - dot(m_ref[ch, c], st_b) + b_ref[ch, c]
        return carry

    lax.fori_loop(0, n_chunks, recur, 0)

    nw = nw_ref[...]
    for ch in range(hpg * rep):
        o = oacc_ref[ch]
        on = o * lax.rsqrt(jnp.mean(o * o, axis=-1, keepdims=True) + HEAD_NORM_EPS) * nw
        zf = z_ref[0, :, ch * d:(ch + 1) * d].astype(F32)
        o_ref[0, :, ch * d:(ch + 1) * d] = (on * (zf * jax.nn.sigmoid(zf))).astype(BF16)


def gdn(qkv, p_a, gc, beta, gc_rows, norm_w, *, tb=512, hpg=4, unroll=4):
    b, s, _ = qkv.shape
    tb = min(tb, s)
    assert s % tb == 0 and tb % (LA_CHUNK * unroll) == 0 and LA_QK_HEADS % hpg == 0
    rep = LA_V_HEADS // LA_QK_HEADS
    d = LA_HEAD_DIM
    qw = hpg * d
    vw = hpg * rep * d
    nch = hpg * rep
    n_chunks = tb // LA_CHUNK
    return pl.pallas_call(
        functools.partial(_gdn_kernel, tb=tb, hpg=hpg, rep=rep, unroll=unroll),
        grid=(b, LA_QK_HEADS // hpg, s // tb),
        in_specs=[
            pl.BlockSpec((1, tb, qw), lambda bi, h, si: (bi, si, h)),
            pl.BlockSpec((1, tb, qw), lambda bi, h, si: (bi, si, LA_QK_DIM // qw + h)),
            pl.BlockSpec((1, tb, vw), lambda bi, h, si: (bi, si, 2 * LA_QK_DIM // vw + h)),
            pl.BlockSpec((1, tb, vw), lambda bi, h, si: (bi, si, LA_CONV_DIM // vw + h)),
            pl.BlockSpec((1, tb, LANES), lambda bi, h, si: (bi, si, 0)),
            pl.BlockSpec((1, tb, LANES), lambda bi, h, si: (bi, si, 0)),
            pl.BlockSpec((1, nch, n_chunks, 2 * LA_CHUNK), lambda bi, h, si: (bi, h, si, 0)),
            pl.BlockSpec((1, d), lambda bi, h, si: (0, 0)),
        ],
        out_specs=pl.BlockSpec((1, tb, vw), lambda bi, h, si: (bi, si, h)),
        out_shape=jax.ShapeDtypeStruct((b, s, LA_V_DIM), BF16),
        scratch_shapes=[
            pltpu.VMEM((nch, d, d), F32),
            pltpu.VMEM((nch, n_chunks, d, d), BF16),
            pltpu.VMEM((nch, n_chunks, d, d), F32),
            pltpu.VMEM((nch, tb, d), BF16),
            pltpu.VMEM((nch, tb, d), F32),
            pltpu.VMEM((nch, n_chunks, LANES), F32),
        ],
        compiler_params=_cparams(("parallel", "parallel", "arbitrary")),
        name="gdn",
    )(qkv, qkv, qkv, p_a, gc, beta, gc_rows, norm_w.reshape(1, d).astype(F32))


def _group_rms(x, gather_ref, spread_ref, width):
    dot = functools.partial(jnp.dot, preferred_element_type=F32)
    sq = _split_bf16(x * x)
    gather = gather_ref[0:width, :]
    ss = dot(sq[0], gather) + dot(sq[1], gather)
    r = _split_bf16(lax.rsqrt(ss * (1.0 / SWA_HEAD_DIM) + HEAD_NORM_EPS))
    spread = spread_ref[:, 0:width]
    return dot(r[0], spread) + dot(r[1], spread)


def _swa_kernel(sink_ref, q_ref, kp_ref, kc_ref, vp_ref, vc_ref, qw_ref, kw_ref, gather_ref, spread_ref,
                 o_ref):
    n = pl.program_id(1)
    blk = WINDOW
    dh = SWA_HEAD_DIM
    slab = 2 * dh
    pairs = SWA_GROUP // 2
    qi = lax.broadcasted_iota(I32, (blk, 2 * blk), 0)
    kj = lax.broadcasted_iota(I32, (blk, 2 * blk), 1)
    dist = qi + blk - kj
    valid = (dist >= 0) & (dist < WINDOW) & ((n > 0) | (kj >= blk))
    dist_or_inf = jnp.where(valid, dist.astype(F32), jnp.inf)
    lane = lax.broadcasted_iota(I32, (2 * blk, slab), 1)
    low = lane < dh

    xq = q_ref[0].astype(F32)
    qn = (xq * _group_rms(xq, gather_ref, spread_ref, SWA_Q_DIM) * qw_ref[...]).astype(BF16)
    xk = jnp.concatenate([kp_ref[0], kc_ref[0]], axis=0).astype(F32)
    kn = xk * _group_rms(xk, gather_ref, spread_ref, SWA_KV_DIM) * kw_ref[...]
    xv = jnp.concatenate([vp_ref[0], vc_ref[0]], axis=0).astype(F32)

    def block_diag(x, g):
        s = x[:, (g // 2) * slab:(g // 2 + 1) * slab]
        other = pltpu.roll(s, dh, 1)
        top, bottom = (s, other) if g % 2 == 0 else (other, s)
        return jnp.concatenate([jnp.where(low, top, 0.0), jnp.where(low, 0.0, bottom)], axis=0).astype(BF16)

    for g in range(SWA_KV_HEADS):
        k_bd = block_diag(kn, g)
        v_bd = block_diag(xv, g)
        q4 = jnp.concatenate([qn[:, (g * pairs + p) * slab:(g * pairs + p + 1) * slab] for p in range(pairs)],
                             axis=0)
        sc = lax.dot_general(q4, k_bd, (((1,), (1,)), ((), ())), preferred_element_type=F32)
        rows = []
        for p in range(pairs):
            halves = []
            for a in range(2):
                h = g * SWA_GROUP + 2 * p + a
                slope = 2.0 ** (-8.0 * (h + 1) / SWA_Q_HEADS)
                sink = sink_ref[h]
                s_h = sc[p * blk:(p + 1) * blk, a * 2 * blk:(a + 1) * 2 * blk] - slope * dist_or_inf
                m = jnp.maximum(jnp.max(s_h, axis=-1, keepdims=True), sink)
                e = jnp.exp(s_h - m)
                denom = jnp.sum(e, axis=-1, keepdims=True) + jnp.exp(sink - m)
                halves.append((e / denom).astype(BF16))
            rows.append(jnp.concatenate(halves, axis=1))
        og = jnp.dot(jnp.concatenate(rows, axis=0), v_bd, preferred_element_type=F32)
        for p in range(pairs):
            c0 = (g * pairs + p) * slab
            o_ref[0, :, c0:c0 + slab] = og[p * blk:(p + 1) * blk, :].astype(BF16)


def swa(p_b, sinks, q_norm_w, k_norm_w):
    b, s, _ = p_b.shape
    blk = WINDOW
    dh = SWA_HEAD_DIM
    assert s % blk == 0
    kblk = SWA_Q_DIM // SWA_KV_DIM
    prev = lambda bi, n: jnp.maximum(n - 1, 0)
    qw = jnp.tile(q_norm_w.astype(F32) * dh ** -0.5, SWA_Q_HEADS).reshape(1, SWA_Q_DIM)
    kw = jnp.tile(k_norm_w.astype(F32), SWA_KV_HEADS).reshape(1, SWA_KV_DIM)
    head_of_col = jnp.arange(SWA_Q_DIM, dtype=I32) // dh
    gather = (head_of_col[:, None] == jnp.arange(LANES, dtype=I32)[None, :]).astype(BF16)
    fixed = lambda bi, n: (0, 0)
    return pl.pallas_call(
        _swa_kernel,
        grid=(b, s // blk),
        in_specs=[
            pl.BlockSpec(memory_space=pltpu.SMEM),
            pl.BlockSpec((1, blk, SWA_Q_DIM), lambda bi, n: (bi, n, 0)),
            pl.BlockSpec((1, blk, SWA_KV_DIM), lambda bi, n: (bi, prev(bi, n), kblk)),
            pl.BlockSpec((1, blk, SWA_KV_DIM), lambda bi, n: (bi, n, kblk)),
            pl.BlockSpec((1, blk, SWA_KV_DIM), lambda bi, n: (bi, prev(bi, n), kblk + 1)),
            pl.BlockSpec((1, blk, SWA_KV_DIM), lambda bi, n: (bi, n, kblk + 1)),
            pl.BlockSpec((1, SWA_Q_DIM), fixed),
            pl.BlockSpec((1, SWA_KV_DIM), fixed),
            pl.BlockSpec((SWA_Q_DIM, LANES), fixed),
            pl.BlockSpec((LANES, SWA_Q_DIM), fixed),
        ],
        out_specs=pl.BlockSpec((1, blk, SWA_Q_DIM), lambda bi, n: (bi, n, 0)),
        out_shape=jax.ShapeDtypeStruct((b, s, SWA_Q_DIM), BF16),
        compiler_params=_cparams(("parallel", "parallel")),
        name="swa",
    )(sinks.astype(F32), p_b, p_b, p_b, p_b, p_b, qw, kw, gather, gather.T)


def _merge_kernel(a_ref, wla_ref, bm_ref, wsw_ref, gla_ref, gsw_ref, bla_ref, bsw_ref, o_ref):
    oa = jnp.dot(a_ref[...], wla_ref[...], preferred_element_type=F32)
    ob = jnp.dot(bm_ref[...], wsw_ref[...], preferred_element_type=F32)
    gl = jax.nn.sigmoid(gla_ref[...].astype(F32) + bla_ref[...])
    gs = jax.nn.sigmoid(gsw_ref[...].astype(F32) + bsw_ref[...])
    o_ref[...] = (gl * oa + gs * ob).astype(BF16)


def merge(o_a, w_la, o_b, w_sw, p_b, b_gate, *, tm=1024, tn=512):
    m, ka = o_a.shape
    kb = o_b.shape[1]
    dm = w_la.shape[1]
    tm = min(tm, m)
    assert m % tm == 0 and dm % tn == 0
    gate0 = (SWA_Q_DIM + 2 * SWA_KV_DIM) // tn
    assert gate0 * tn == SWA_Q_DIM + 2 * SWA_KV_DIM
    nblk = dm // tn
    bg = b_gate.astype(F32).reshape(1, 2 * dm)
    return pl.pallas_call(
        _merge_kernel,
        grid=(m // tm, nblk),
        in_specs=[
            pl.BlockSpec((tm, ka), lambda i, j: (i, 0)),
            pl.BlockSpec((ka, tn), lambda i, j: (0, j)),
            pl.BlockSpec((tm, kb), lambda i, j: (i, 0)),
            pl.BlockSpec((kb, tn), lambda i, j: (0, j)),
            pl.BlockSpec((tm, tn), lambda i, j: (i, gate0 + j)),
            pl.BlockSpec((tm, tn), lambda i, j: (i, gate0 + nblk + j)),
            pl.BlockSpec((1, tn), lambda i, j: (0, j)),
            pl.BlockSpec((1, tn), lambda i, j: (0, nblk + j)),
        ],
        out_specs=pl.BlockSpec((tm, tn), lambda i, j: (i, j)),
        out_shape=jax.ShapeDtypeStruct((m, dm), BF16),
        compiler_params=_cparams(("parallel", "arbitrary")),
        name="merge",
    )(o_a, w_la, o_b, w_sw, p_b, p_b, bg, bg)


def _pack_rows(x):
    bits = lax.bitcast_convert_type(x.astype(BF16).astype(F32), U32)
    half = x.shape[1] // 2
    return (bits[:, :half] >> U32(16)) | (bits[:, half:] & U32(HIGH_HALF))


def _unpack_rows(w):
    lo = lax.bitcast_convert_type(w << U32(16), F32)
    hi = lax.bitcast_convert_type(w & U32(HIGH_HALF), F32)
    return lo, hi


def _oproj_router_kernel(x_ref, m_ref, wo_ref, g_ref, wr_ref, br_ref,
                         x1_ref, h2_ref, idx_ref, wgt_ref, rank_ref, cnt_ref, carry_ref, *, tm):
    @pl.when(pl.program_id(0) == 0)
    def _():
        carry_ref[...] = jnp.zeros_like(carry_ref)

    x1 = x_ref[...] + jnp.dot(m_ref[...], wo_ref[...], preferred_element_type=F32)
    x1_ref[...] = x1
    var = jnp.mean(x1 * x1, axis=-1, keepdims=True)
    h2 = x1 * lax.rsqrt(var + NORM_EPS) * g_ref[...]
    h2_ref[...] = _pack_rows(h2)
    lane = lax.broadcasted_iota(I32, (tm, LANES), 1)
    logits = jnp.dot(h2.astype(BF16), wr_ref[...], preferred_element_type=F32) + br_ref[...]
    logits = jnp.where(lane < N_EXPERTS, logits, -jnp.inf)
    vals, idxs = [], []
    for _ in range(TOP_K):
        mx = jnp.max(logits, axis=-1, keepdims=True)
        ix = jnp.min(jnp.where(logits == mx, lane, LANES), axis=-1, keepdims=True)
        vals.append(mx)
        idxs.append(ix)
        logits = jnp.where(lane == ix, -jnp.inf, logits)
    exps = [jnp.exp(v - vals[0]) for v in vals]
    tot = exps[0]
    for e in exps[1:]:
        tot = tot + e
    multihot = jnp.zeros((tm, LANES), F32)
    for ix in idxs:
        multihot = multihot + jnp.where(lane == ix, 1.0, 0.0)
    row = lax.broadcasted_iota(I32, (tm, tm), 0)
    col = lax.broadcasted_iota(I32, (tm, tm), 1)
    lower = jnp.where(col < row, 1.0, 0.0).astype(BF16)
    before = jnp.dot(lower, multihot.astype(BF16), preferred_element_type=F32) + carry_ref[...]
    idx_o = jnp.zeros((tm, LANES), I32)
    wgt_o = jnp.zeros((tm, LANES), F32)
    rank_o = jnp.zeros((tm, LANES), I32)
    for kk in range(TOP_K):
        rk = jnp.sum(jnp.where(lane == idxs[kk], before, 0.0), axis=-1, keepdims=True)
        idx_o = jnp.where(lane == kk, idxs[kk], idx_o)
        wgt_o = jnp.where(lane == kk, exps[kk] / tot, wgt_o)
        rank_o = jnp.where(lane == kk, rk.astype(I32), rank_o)
    idx_ref[...] = idx_o
    wgt_ref[...] = wgt_o
    rank_ref[...] = rank_o
    carry = carry_ref[...] + jnp.sum(multihot, axis=0, keepdims=True)
    carry_ref[...] = carry
    cnt_ref[...] = carry.astype(I32)


def oproj_router(x, m_act, w_o, gain, w_router, b_router, *, tm=256):
    m, dm = x.shape
    tm = min(tm, m)
    assert m % tm == 0
    wr = jnp.pad(w_router.astype(BF16), ((0, 0), (0, LANES - N_EXPERTS)))
    br = jnp.pad(b_router.astype(F32), (0, LANES - N_EXPERTS)).reshape(1, LANES)
    tok = lambda i: (i, 0)
    fixed = lambda i: (0, 0)
    return pl.pallas_call(
        functools.partial(_oproj_router_kernel, tm=tm),
        grid=(m // tm,),
        in_specs=[
            pl.BlockSpec((tm, dm), tok),
            pl.BlockSpec((tm, dm), tok),
            pl.BlockSpec((dm, dm), fixed),
            pl.BlockSpec((1, dm), fixed),
            pl.BlockSpec((dm, LANES), fixed),
            pl.BlockSpec((1, LANES), fixed),
        ],
        out_specs=[
            pl.BlockSpec((tm, dm), tok),
            pl.BlockSpec((tm, dm // 2), tok),
            pl.BlockSpec((tm, LANES), tok),
            pl.BlockSpec((tm, LANES), tok),
            pl.BlockSpec((tm, LANES), tok),
            pl.BlockSpec((1, LANES), fixed),
        ],
        out_shape=[
            jax.ShapeDtypeStruct((m, dm), F32),
            jax.ShapeDtypeStruct((m, dm // 2), U32),
            jax.ShapeDtypeStruct((m, LANES), I32),
            jax.ShapeDtypeStruct((m, LANES), F32),
            jax.ShapeDtypeStruct((m, LANES), I32),
            jax.ShapeDtypeStruct((1, LANES), I32),
        ],
        scratch_shapes=[pltpu.VMEM((1, LANES), F32)],
        compiler_params=_cparams(("arbitrary",)),
        name="oproj_router",
    )(x, m_act, w_o, gain.reshape(1, dm).astype(F32), wr, br)


def _row_copy(src, dst, sem, src_row, dst_row):
    return pltpu.make_async_copy(src.at[pl.ds(src_row, 1), :], dst.at[pl.ds(dst_row, 1), :], sem)


def _dispatch_kernel(pos_ref, h_ref, zero_hbm, xs_hbm, sem, *, td):
    del zero_hbm

    def issue(i, carry):
        for kk in range(TOP_K):
            _row_copy(h_ref, xs_hbm, sem, i, pos_ref[i * TOP_K + kk]).start()
        return carry

    lax.fori_loop(0, td, issue, 0, unroll=4)

    def drain(i, carry):
        _row_copy(h_ref, xs_hbm, sem, 0, 0).wait()
        return carry

    lax.fori_loop(0, td * TOP_K, drain, 0, unroll=8)


def dispatch(h_rows, pos_flat, n_rows, *, td=512):
    m, width = h_rows.shape
    td = min(td, m)
    assert m % td == 0
    zeros = jnp.zeros((n_rows, width), h_rows.dtype)
    return pl.pallas_call(
        functools.partial(_dispatch_kernel, td=td),
        grid=(m // td,),
        in_specs=[
            pl.BlockSpec((td * TOP_K,), lambda i: (i,), memory_space=pltpu.SMEM),
            pl.BlockSpec((td, width), lambda i: (i, 0)),
            pl.BlockSpec(memory_space=pl.ANY),
        ],
        out_specs=pl.BlockSpec(memory_space=pl.ANY),
        out_shape=jax.ShapeDtypeStruct(zeros.shape, zeros.dtype),
        scratch_shapes=[pltpu.SemaphoreType.DMA(())],
        input_output_aliases={2: 0},
        compiler_params=_cparams(("arbitrary",)),
        name="dispatch",
    )(pos_flat, h_rows, zeros)


def _deinterleave_kernel(w_ref, g_ref, l_ref):
    half = MXU_TILE // 2
    r = lax.broadcasted_iota(I32, (MXU_TILE, MXU_TILE), 0)
    c = lax.broadcasted_iota(I32, (MXU_TILE, MXU_TILE), 1)
    perm = jnp.where(r == jnp.where(c < half, 2 * c, 2 * (c - half) + 1), 1.0, 0.0).astype(BF16)
    for blk in range(w_ref.shape[2] // MXU_TILE):
        piece = w_ref[0, :, blk * MXU_TILE:(blk + 1) * MXU_TILE].astype(BF16)
        out = jnp.dot(piece, perm, preferred_element_type=F32)
        g_ref[0, :, blk * half:(blk + 1) * half] = out[:, :half].astype(BF16)
        l_ref[0, :, blk * half:(blk + 1) * half] = out[:, half:].astype(BF16)


def deinterleave(w, *, tk=1024, tn=1024):
    e, k, n2 = w.shape
    n = n2 // 2
    assert k % tk == 0 and n % tn == 0 and (2 * tn) % MXU_TILE == 0
    out = jax.ShapeDtypeStruct((e, k, n), BF16)
    return pl.pallas_call(
        _deinterleave_kernel,
        grid=(e, k // tk, n // tn),
        in_specs=[pl.BlockSpec((1, tk, 2 * tn), lambda ei, ki, ni: (ei, ki, ni))],
        out_specs=[pl.BlockSpec((1, tk, tn), lambda ei, ki, ni: (ei, ki, ni)),
                   pl.BlockSpec((1, tk, tn), lambda ei, ki, ni: (ei, ki, ni))],
        out_shape=[out, out],
        compiler_params=_cparams(("parallel", "parallel", "parallel")),
        name="deinterleave",
    )(w)


def _experts_kernel(be_ref, nv_ref, x_ref, w1g_ref, w1l_ref, b1g_ref, b1l_ref, w2_ref, b2_ref,
                    o_ref, acc_ref, xb_ref, act_ref, *, nj):
    i = pl.program_id(0)
    j = pl.program_id(1)
    live = i < nv_ref[0]
    half = x_ref.shape[1]

    @pl.when(live & (j == 0))
    def _():
        lo, hi = _unpack_rows(x_ref[...])
        xb_ref[:, :half] = lo.astype(BF16)
        xb_ref[:, half:] = hi.astype(BF16)

    @pl.when(live)
    def _():
        x = xb_ref[...]
        hg = jnp.dot(x, w1g_ref[0], preferred_element_type=F32) + b1g_ref[0]
        hl = jnp.dot(x, w1l_ref[0], preferred_element_type=F32) + b1l_ref[0]
        glu = jnp.minimum(hg, SWIGLU_LIMIT)
        lin = jnp.clip(hl, -SWIGLU_LIMIT, SWIGLU_LIMIT)
        act_ref[...] = (glu * jax.nn.sigmoid(SWIGLU_ALPHA * glu) * (lin + 1.0)).astype(BF16)

    down = lambda: jnp.dot(act_ref[...], w2_ref[0], preferred_element_type=F32)
    first = live & (j == 0)
    last = live & (j == nj - 1)
    if nj == 1:
        @pl.when(first)
        def _():
            o_ref[...] = _pack_rows(down() + b2_ref[0])
    else:
        @pl.when(first)
        def _():
            acc_ref[...] = down() + b2_ref[0]

        @pl.when(live & (j > 0) & (j < nj - 1))
        def _():
            acc_ref[...] = acc_ref[...] + down()

        @pl.when(last)
        def _():
            o_ref[...] = _pack_rows(acc_ref[...] + down())

    @pl.when(jnp.logical_not(live) & (j == nj - 1))
    def _():
        o_ref[...] = jnp.zeros_like(o_ref)


def experts(xs, blk_expert, n_live, w1g, w1l, b1g, b1l, w2, b2, *, tj=1024):
    n_rows, half = xs.shape
    dm = 2 * half
    n_blk = n_rows // MOE_ROWS
    de = w2.shape[1]
    assert de % tj == 0
    nj = de // tj

    def wsel(i, j, be, nv):
        return be[i], jnp.where(i < nv[0], j, nj - 1)

    grid_spec = pltpu.PrefetchScalarGridSpec(
        num_scalar_prefetch=2,
        grid=(n_blk, nj),
        in_specs=[
            pl.BlockSpec((MOE_ROWS, half), lambda i, j, be, nv: (jnp.clip(i, 0, jnp.maximum(nv[0] - 1, 0)), 0)),
            pl.BlockSpec((1, dm, tj), lambda i, j, be, nv: (wsel(i, j, be, nv)[0], 0, wsel(i, j, be, nv)[1])),
            pl.BlockSpec((1, dm, tj), lambda i, j, be, nv: (wsel(i, j, be, nv)[0], 0, wsel(i, j, be, nv)[1])),
            pl.BlockSpec((1, 1, tj), lambda i, j, be, nv: (wsel(i, j, be, nv)[0], 0, wsel(i, j, be, nv)[1])),
            pl.BlockSpec((1, 1, tj), lambda i, j, be, nv: (wsel(i, j, be, nv)[0], 0, wsel(i, j, be, nv)[1])),
            pl.BlockSpec((1, tj, dm), lambda i, j, be, nv: (wsel(i, j, be, nv)[0], wsel(i, j, be, nv)[1], 0)),
            pl.BlockSpec((1, 1, dm), lambda i, j, be, nv: (be[i], 0, 0)),
        ],
        out_specs=pl.BlockSpec((MOE_ROWS, half), lambda i, j, be, nv: (i, 0)),
        scratch_shapes=[pltpu.VMEM((MOE_ROWS, dm), F32), pltpu.VMEM((MOE_ROWS, dm), BF16),
                        pltpu.VMEM((MOE_ROWS, tj), BF16)],
    )
    return pl.pallas_call(
        functools.partial(_experts_kernel, nj=nj),
        grid_spec=grid_spec,
        out_shape=jax.ShapeDtypeStruct((n_rows, half), U32),
        compiler_params=_cparams(("arbitrary", "arbitrary")),
        name="experts",
    )(blk_expert, n_live, xs, w1g, w1l, b1g, b1l, w2, b2)


def _combine_kernel(pos_ref, pos_next_ref, x1_ref, wgt_ref, y_hbm, o_ref, ybuf_ref, sem, *, tt):
    i = pl.program_id(0)
    slot = lax.rem(i, 2)
    half = ybuf_ref.shape[-1]

    def gather(p_ref, sl):
        def issue(t, carry):
            for kk in range(TOP_K):
                _row_copy(y_hbm, ybuf_ref.at[sl, kk], sem.at[sl], p_ref[t * TOP_K + kk], t).start()
            return carry

        lax.fori_loop(0, tt, issue, 0, unroll=4)

    @pl.when(i == 0)
    def _():
        gather(pos_ref, 0)

    @pl.when(i + 1 < pl.num_programs(0))
    def _():
        gather(pos_next_ref, 1 - slot)

    def drain(r, carry):
        _row_copy(y_hbm, ybuf_ref.at[slot, 0], sem.at[slot], 0, 0).wait()
        return carry

    lax.fori_loop(0, tt * TOP_K, drain, 0, unroll=8)

    w = wgt_ref[...]
    acc_lo = x1_ref[:, :half]
    acc_hi = x1_ref[:, half:]
    for kk in range(TOP_K):
        lo, hi = _unpack_rows(ybuf_ref[slot, kk])
        wk = w[:, kk:kk + 1]
        acc_lo = acc_lo + wk * lo
        acc_hi = acc_hi + wk * hi
    o_ref[:, :half] = acc_lo
    o_ref[:, half:] = acc_hi


def combine(x1, y_rows, pos_flat, wgt, *, tt=256):
    m, dm = x1.shape
    half = y_rows.shape[1]
    tt = min(tt, m)
    assert m % tt == 0 and dm == 2 * half
    n_tiles = m // tt
    return pl.pallas_call(
        functools.partial(_combine_kernel, tt=tt),
        grid=(n_tiles,),
        in_specs=[
            pl.BlockSpec((tt * TOP_K,), lambda i: (i,), memory_space=pltpu.SMEM),
            pl.BlockSpec((tt * TOP_K,), lambda i: (jnp.minimum(i + 1, n_tiles - 1),), memory_space=pltpu.SMEM),
            pl.BlockSpec((tt, dm), lambda i: (i, 0)),
            pl.BlockSpec((tt, LANES), lambda i: (i, 0)),
            pl.BlockSpec(memory_space=pl.ANY),
        ],
        out_specs=pl.BlockSpec((tt, dm), lambda i: (i, 0)),
        out_shape=jax.ShapeDtypeStruct((m, dm), F32),
        scratch_shapes=[pltpu.VMEM((2, TOP_K, tt, half), y_rows.dtype), pltpu.SemaphoreType.DMA((2,))],
        compiler_params=_cparams(("arbitrary",)),
        name="combine",
    )(pos_flat, pos_flat, x1, wgt, y_rows)


def _layer(x, ln_mix_w, w_in, b_gate, conv_w, a_log, dt_bias, la_norm_w, w_out_la, q_norm_w,
           k_norm_w, sinks, w_out_swa, w_o, ln_ffn_w, w_router, b_router, w1, b1, w2, b2):
    bsz, seq, dm = x.shape
    n_tok = bsz * seq
    xf = x.reshape(n_tok, dm)

    c_z_end = LA_CONV_DIM + LA_V_DIM
    c_ba_end = c_z_end + 2 * LA_V_HEADS
    w_all = w_in.astype(BF16)
    w_b = w_all[:, c_ba_end:]
    lane_pad = ((0, 0), (0, LANES - LA_V_HEADS))
    w_c = jnp.concatenate([jnp.pad(w_all[:, c_z_end:c_z_end + LA_V_HEADS], lane_pad),
                           jnp.pad(w_all[:, c_z_end + LA_V_HEADS:c_ba_end], lane_pad)], axis=1)

    p_a = rms_matmul(xf, ln_mix_w, w_all, out_dtype=BF16, tm=1024, tn=1024, n=c_z_end)
    p_b = rms_matmul(xf, ln_mix_w, w_b, out_dtype=BF16, tm=1024, tn=512)
    p_c = rms_matmul(xf, ln_mix_w, w_c, out_dtype=F32, tm=1024, tn=2 * LANES)

    p_a3 = p_a.reshape(bsz, seq, c_z_end)
    qkv = conv_prep(p_a3, conv_w)
    beta, gc = gates(p_c, a_log, dt_bias)
    gc_rows = gc[:, :LA_V_HEADS].reshape(bsz, seq // LA_CHUNK, LA_CHUNK, LA_V_HEADS).transpose(0, 3, 1, 2)
    gc_rows = jnp.concatenate([gc_rows, gc_rows], axis=-1)
    o_a = gdn(qkv, p_a3, gc.reshape(bsz, seq, LANES), beta.reshape(bsz, seq, LANES), gc_rows, la_norm_w)

    p_b3 = p_b.reshape(bsz, seq, p_b.shape[1])
    o_b = swa(p_b3, sinks, q_norm_w, k_norm_w)

    m_act = merge(o_a.reshape(n_tok, LA_V_DIM), w_out_la.astype(BF16),
                  o_b.reshape(n_tok, SWA_Q_DIM), w_out_swa.astype(BF16), p_b, b_gate)
    x1, h2, idx, wgt, rank, counts = oproj_router(xf, m_act, w_o.astype(BF16), ln_ffn_w, w_router, b_router)

    counts = counts[0, :N_EXPERTS]
    padded = (counts + MOE_ROWS - 1) // MOE_ROWS * MOE_ROWS
    pend = jnp.cumsum(padded)
    pstart = pend - padded
    n_blk = n_tok * TOP_K // MOE_ROWS + N_EXPERTS
    n_rows = n_blk * MOE_ROWS
    n_live = (pend[-1:] // MOE_ROWS).astype(I32)
    blk_start = jnp.minimum(jnp.arange(n_blk, dtype=I32), jnp.maximum(n_live[0] - 1, 0)) * MOE_ROWS
    blk_expert = jnp.minimum(jnp.sum(pend[None, :] <= blk_start[:, None], axis=1), N_EXPERTS - 1).astype(I32)
    pos_flat = (pstart[idx[:, :TOP_K]] + rank[:, :TOP_K]).reshape(n_tok * TOP_K).astype(I32)

    xs = dispatch(h2, pos_flat, n_rows)
    de = w2.shape[1]
    w1g, w1l = deinterleave(w1)
    y = experts(xs, blk_expert, n_live, w1g, w1l,
                b1[:, 0::2].reshape(N_EXPERTS, 1, de).astype(F32), b1[:, 1::2].reshape(N_EXPERTS, 1, de).astype(F32),
                w2.astype(BF16), b2.reshape(N_EXPERTS, 1, dm).astype(F32))
    out = combine(x1, y, pos_flat, wgt)
    return out.reshape(bsz, seq, dm)


def kernel(x, ln_mix_w, w_in, b_gate, conv_w, a_log, dt_bias, la_norm_w, w_out_la, q_norm_w, k_norm_w,
           sinks, w_out_swa, w_o, ln_ffn_w, w_router, b_router, w1, b1, w2, b2):
    params = (ln_mix_w, w_in, b_gate, conv_w, a_log, dt_bias, la_norm_w, w_out_la, q_norm_w, k_norm_w,
              sinks, w_out_swa, w_o, ln_ffn_w, w_router, b_router, w1, b1, w2, b2)
    for layer in range(ln_mix_w.shape[0]):
        x = _layer(x, *(p[layer] for p in params))
    return x
```
